```python
import math
import jax, jax.numpy as jnp
from jax import lax
import numpy as np

D_MODEL = 1024
BATCH = 16
SEQ = 2048
DEPTH = 1

RMS_EPS = 1e-6
D_FF = 2816
BLOCK = 128
MLA_HEADS = 8
MLA_Q_RANK = 256
MLA_KV_RANK = 128
MLA_NOPE = 64
MLA_ROPE = 32
MLA_V = 64
ROPE_THETA = 10000.0
SWA_HEADS = 8
SWA_KV_HEADS = 2
SWA_HEAD_DIM = 64
SWA_GROUP = SWA_HEADS // SWA_KV_HEADS
WINDOW = 128
N_SIDE = -(-WINDOW // BLOCK)
PAD = N_SIDE * BLOCK
KEY_SPAN = (2 * N_SIDE + 1) * BLOCK

MLA_OUT = MLA_HEADS * MLA_V
SWA_OUT = SWA_HEADS * SWA_HEAD_DIM
MIX_WIDTH = MLA_OUT + SWA_OUT
IN_SPLITS = [MLA_Q_RANK, MLA_KV_RANK, MLA_ROPE,
             SWA_HEADS * SWA_HEAD_DIM, SWA_KV_HEADS * SWA_HEAD_DIM, SWA_KV_HEADS * SWA_HEAD_DIM]
IN_WIDTH = sum(IN_SPLITS)

kernel_name = "hybrid_mla_swa_macaron_encoder_layer"


def rms_norm(x, g, eps=RMS_EPS):
    xf = x.astype(jnp.float32)
    y = xf * lax.rsqrt(jnp.mean(xf * xf, axis=-1, keepdims=True) + eps)
    return (y * g.astype(jnp.float32)).astype(x.dtype)


def swiglu(x, w_gate, w_up, w_down):
    return (jax.nn.silu(x @ w_gate) * (x @ w_up)) @ w_down


def rope_tables(seq_len, dim):
    pos = jnp.arange(seq_len, dtype=jnp.float32)
    inv = 1.0 / (ROPE_THETA ** (jnp.arange(0, dim, 2, dtype=jnp.float32) / dim))
    ang = pos[:, None] * inv[None, :]
    return jnp.cos(ang), jnp.sin(ang)


def apply_rope(x, cos, sin):
    xf = x.astype(jnp.float32)
    x1, x2 = jnp.split(xf, 2, axis=-1)
    return jnp.concatenate([x1 * cos - x2 * sin, x2 * cos + x1 * sin], axis=-1).astype(x.dtype)


def mla_mixer(hq, hkv, hkr, g_q_a, w_uq, g_kv_a, w_ukv, g_qn, g_qr, g_kn, g_kr):
    B, S, _ = hq.shape
    nb = S // BLOCK
    cos, sin = rope_tables(S, MLA_ROPE)
    c_q = rms_norm(hq, g_q_a)
    q = (c_q @ w_uq).reshape(B, S, MLA_HEADS, MLA_NOPE + MLA_ROPE)
    q_nope = rms_norm(q[..., :MLA_NOPE], g_qn)
    q_pe = apply_rope(rms_norm(q[..., MLA_NOPE:], g_qr), cos[:, None], sin[:, None])
    c_kv = rms_norm(hkv, g_kv_a)
    kv = (c_kv @ w_ukv).reshape(B, S, MLA_HEADS, MLA_NOPE + MLA_V)
    k_nope = rms_norm(kv[..., :MLA_NOPE], g_kn)
    v = kv[..., MLA_NOPE:]
    k_pe = apply_rope(rms_norm(hkr, g_kr), cos, sin)
    scale = 1.0 / math.sqrt(MLA_NOPE + MLA_ROPE)
    qn_b = jnp.moveaxis(q_nope.reshape(B, nb, BLOCK, MLA_HEADS, MLA_NOPE), 1, 0)
    qp_b = jnp.moveaxis(q_pe.reshape(B, nb, BLOCK, MLA_HEADS, MLA_ROPE), 1, 0)

    def one_block(args):
        qn, qp = args
        s = (jnp.einsum('bqhd,bkhd->bhqk', qn, k_nope).astype(jnp.float32)
             + jnp.einsum('bqhr,bkr->bhqk', qp, k_pe).astype(jnp.float32)) * scale
        p = jax.nn.softmax(s, axis=-1)
        return jnp.einsum('bhqk,bkhd->bqhd', p.astype(v.dtype), v)

    out = lax.map(one_block, (qn_b, qp_b))
    return jnp.moveaxis(out, 0, 1).reshape(B, S, MLA_OUT)


def swa_mixer(q, k, v, g_q, g_k, sink):
    B, S, _ = q.shape
    nb = S // BLOCK
    D = SWA_HEAD_DIM
    q = rms_norm(q.reshape(B, S, SWA_HEADS, D), g_q)
    k = rms_norm(k.reshape(B, S, SWA_KV_HEADS, D), g_k)
    v = v.reshape(B, S, SWA_KV_HEADS, D)
    qb = q.reshape(B, nb, BLOCK, SWA_KV_HEADS, SWA_GROUP, D)
    kp = jnp.pad(k, ((0, 0), (PAD, PAD), (0, 0), (0, 0)))
    vp = jnp.pad(v, ((0, 0), (PAD, PAD), (0, 0), (0, 0)))
    kb = jnp.concatenate([kp[:, j * BLOCK:j * BLOCK + S].reshape(B, nb, BLOCK, SWA_KV_HEADS, D)
                          for j in range(2 * N_SIDE + 1)], axis=2)
    vb = jnp.concatenate([vp[:, j * BLOCK:j * BLOCK + S].reshape(B, nb, BLOCK, SWA_KV_HEADS, D)
                          for j in range(2 * N_SIDE + 1)], axis=2)
    qi = jnp.arange(BLOCK)[:, None]
    kj = jnp.arange(KEY_SPAN)[None, :]
    rel = kj - PAD - qi
    s_abs = jnp.arange(nb)[:, None, None] * BLOCK - PAD + kj[None]
    valid = (jnp.abs(rel)[None] <= WINDOW) & (s_abs >= 0) & (s_abs < S)
    slopes = 2.0 ** (-(8.0 / SWA_HEADS) * jnp.arange(1, SWA_HEADS + 1, dtype=jnp.float32))
    alibi = -slopes.reshape(SWA_KV_HEADS, SWA_GROUP)[:, :, None, None, None] * \
        jnp.abs(rel).astype(jnp.float32)[None, None, None]
    logits = jnp.einsum('bnqhgd,bnkhd->bhgnqk', qb, kb).astype(jnp.float32) / math.sqrt(D)
    logits = jnp.where(valid, logits + alibi[None], -jnp.inf)
    sk = sink.astype(jnp.float32).reshape(SWA_KV_HEADS, SWA_GROUP)[None, :, :, None, None, None]
    m = jnp.maximum(jnp.max(logits, axis=-1, keepdims=True), sk)
    p = jnp.exp(logits - m)
    w = p / (jnp.sum(p, axis=-1, keepdims=True) + jnp.exp(sk - m))
    out = jnp.einsum('bhgnqk,bnkhd->bnqhgd', w.astype(vb.dtype), vb)
    return out.reshape(B, S, SWA_OUT)


def setup_inputs(seed: int = 0) -> dict:
    key = jax.random.key(seed)
    ks = jax.random.split(key, 24)

    def w(k, shape):
        return jax.random.normal(k, shape, jnp.float32) * shape[0] ** -0.5

    def g(k, n):
        return 1.0 + 0.02 * jax.random.normal(k, (n,), jnp.float32)

    return {
        "x": jax.random.normal(ks[0], (BATCH, SEQ, D_MODEL), jnp.float32),
        "g_ffn1": g(ks[1], D_MODEL),
        "w1_gate": w(ks[2], (D_MODEL, D_FF)),
        "w1_up": w(ks[3], (D_MODEL, D_FF)),
        "w1_down": w(ks[4], (D_FF, D_MODEL)),
        "g_mix": g(ks[5], D_MODEL),
        "w_in": w(ks[6], (D_MODEL, IN_WIDTH)),
        "g_q_a": g(ks[7], MLA_Q_RANK),
        "w_uq": w(ks[8], (MLA_Q_RANK, MLA_HEADS * (MLA_NOPE + MLA_ROPE))),
        "g_kv_a": g(ks[9], MLA_KV_RANK),
        "w_ukv": w(ks[10], (MLA_KV_RANK, MLA_HEADS * (MLA_NOPE + MLA_V))),
        "g_mla_qn": g(ks[11], MLA_NOPE),
        "g_mla_qr": g(ks[12], MLA_ROPE),
        "g_mla_kn": g(ks[13], MLA_NOPE),
        "g_mla_kr": g(ks[14], MLA_ROPE),
        "g_swa_q": g(ks[15], SWA_HEAD_DIM),
        "g_swa_k": g(ks[16], SWA_HEAD_DIM),
        "sink": 0.5 * jax.random.normal(ks[17], (SWA_HEADS,), jnp.float32),
        "w_o": w(ks[18], (MIX_WIDTH, D_MODEL)),
        "g_ffn2": g(ks[19], D_MODEL),
        "w2_gate": w(ks[20], (D_MODEL, D_FF)),
        "w2_up": w(ks[21], (D_MODEL, D_FF)),
        "w2_down": w(ks[22], (D_FF, D_MODEL)),
    }


def reference(x, g_ffn1, w1_gate, w1_up, w1_down, g_mix, w_in, g_q_a, w_uq, g_kv_a, w_ukv,
              g_mla_qn, g_mla_qr, g_mla_kn, g_mla_kr, g_swa_q, g_swa_k, sink, w_o,
              g_ffn2, w2_gate, w2_up, w2_down):
    split_idx = list(np.cumsum(IN_SPLITS)[:-1])
    for _ in range(DEPTH):
        x = x + 0.5 * swiglu(rms_norm(x, g_ffn1), w1_gate, w1_up, w1_down)
        h = rms_norm(x, g_mix)
        hq, hkv, hkr, sq, sk_, sv = jnp.split(h @ w_in, split_idx, axis=-1)
        y_a = mla_mixer(hq, hkv, hkr, g_q_a, w_uq, g_kv_a, w_ukv,
                        g_mla_qn, g_mla_qr, g_mla_kn, g_mla_kr)
        y_b = swa_mixer(sq, sk_, sv, g_swa_q, g_swa_k, sink)
        x = x + jnp.concatenate([y_a, y_b], axis=-1) @ w_o
        x = x + 0.5 * swiglu(rms_norm(x, g_ffn2), w2_gate, w2_up, w2_down)
    return x
```

```python
import math

import jax
import jax.numpy as jnp
from jax import lax
from jax.experimental import pallas as pl
from jax.experimental.pallas import tpu as pltpu

F32 = jnp.float32
BF16 = jnp.bfloat16

RMS_EPS = 1e-6
MLA_HEADS = 8
MLA_NOPE = 64
MLA_ROPE = 32
MLA_V = 64
ROPE_THETA = 10000.0
SWA_HEADS = 8
SWA_KV_HEADS = 2
SWA_HEAD_DIM = 64
SWA_GROUP = SWA_HEADS // SWA_KV_HEADS
WINDOW = 128

LANES = 128
V7X_VMEM_BYTES = 64 * 1024 * 1024
VMEM_HEADROOM_BYTES = 8 * 1024 * 1024

FFN_ROWS = 512
FFN_CHUNK = 256
PROJ_ROWS = 512
MLA_Q_ROWS = 256
SWA_Q_ROWS = 128
SWA_KEYS = SWA_Q_ROWS + 2 * WINDOW


def _vmem_limit(block_bytes):
    want = 2 * block_bytes + 16 * 1024 * 1024
    return int(min(want, V7X_VMEM_BYTES - VMEM_HEADROOM_BYTES))


def _nbytes(shape, dtype):
    return math.prod(shape) * jnp.dtype(dtype).itemsize


def _const_spec(shape):
    zeros = (0,) * len(shape)
    return pl.BlockSpec(shape, lambda *_: zeros)


def _rms(x, gain):
    ms = jnp.mean(x * x, axis=-1, keepdims=True)
    return x * lax.rsqrt(ms + RMS_EPS) * gain


def _dot(a, b):
    return jnp.dot(a, b, preferred_element_type=F32)


def _dot_nt(a, b):
    return lax.dot_general(a, b, (((1,), (1,)), ((), ())), preferred_element_type=F32)


def _swiglu_residual(x, gain_ref, wg_ref, wu_ref, wd_ref, act_ref):
    h = _rms(x, gain_ref[...]).astype(BF16)
    d_ff = wg_ref.shape[1]
    for c in range(d_ff // FFN_CHUNK):
        cols = slice(c * FFN_CHUNK, (c + 1) * FFN_CHUNK)
        gate = _dot(h, wg_ref[:, cols])
        up = _dot(h, wu_ref[:, cols])
        act_ref[:, cols] = (gate / (1.0 + jnp.exp(-gate)) * up).astype(BF16)
    return x + 0.5 * _dot(act_ref[...], wd_ref[...])


def _ffn_kernel(x_ref, gain_ref, wg_ref, wu_ref, wd_ref, o_ref, act_ref):
    o_ref[...] = _swiglu_residual(x_ref[...], gain_ref, wg_ref, wu_ref, wd_ref, act_ref)


def _mix_ffn_kernel(x_ref, ya_ref, yb_ref, wo_ref, gain_ref, wg_ref, wu_ref, wd_ref,
                    o_ref, act_ref):
    half = ya_ref.shape[1]
    x2 = (x_ref[...] + _dot(ya_ref[...], wo_ref[:half, :])
          + _dot(yb_ref[...], wo_ref[half:, :]))
    o_ref[...] = _swiglu_residual(x2, gain_ref, wg_ref, wu_ref, wd_ref, act_ref)


def _ffn_call(x, gain, wg, wu, wd, mix=None):
    t, d = x.shape
    d_ff = wg.shape[1]
    assert t % FFN_ROWS == 0 and d_ff % FFN_CHUNK == 0
    row = lambda i: (i, 0)
    x_spec = pl.BlockSpec((FFN_ROWS, d), row)
    w_specs = [_const_spec((1, d)), _const_spec((d, d_ff)), _const_spec((d, d_ff)),
               _const_spec((d_ff, d))]
    block_bytes = (2 * _nbytes((FFN_ROWS, d), F32) + 3 * _nbytes((d, d_ff), BF16)
                   + _nbytes((FFN_ROWS, d_ff), BF16))
    if mix is None:
        body, operands, in_specs = _ffn_kernel, (x, gain, wg, wu, wd), [x_spec] + w_specs
    else:
        ya, yb, wo = mix
        half = ya.shape[1]
        y_spec = pl.BlockSpec((FFN_ROWS, half), row)
        body = _mix_ffn_kernel
        operands = (x, ya, yb, wo, gain, wg, wu, wd)
        in_specs = [x_spec, y_spec, y_spec, _const_spec(wo.shape)] + w_specs
        block_bytes += 2 * _nbytes((FFN_ROWS, half), BF16) + _nbytes(wo.shape, BF16)
    return pl.pallas_call(
        body,
        out_shape=jax.ShapeDtypeStruct((t, d), F32),
        grid=(t // FFN_ROWS,),
        in_specs=in_specs,
        out_specs=x_spec,
        scratch_shapes=[pltpu.VMEM((FFN_ROWS, d_ff), BF16)],
        compiler_params=pltpu.CompilerParams(
            dimension_semantics=("parallel",), vmem_limit_bytes=_vmem_limit(block_bytes)),
        name="ffn" if mix is None else "mix_ffn",
    )(*operands)


def _segment_mean_sq(x, seg_ref):
    sq = x * x
    hi = sq.astype(BF16)
    lo = (sq - hi.astype(F32)).astype(BF16)
    seg = seg_ref[...]
    return _dot(hi, seg) + _dot(lo, seg)


def _proj_kernel(x_ref, gmix_ref, win_ref, gqa_ref, wuq_ref, gkva_ref, wk_ref, wv_ref,
                 vones_ref, gq_ref, gk_ref, gkr_ref, gsq_ref, gsk_ref,
                 seg_mla_ref, seg_swa_ref, rc_ref, ra_ref, rb_ref,
                 q_ref, k_ref, v_ref, sq_ref, sk_ref, sv_ref):
    h = _rms(x_ref[...], gmix_ref[...]).astype(BF16)
    rc, ra, rb = rc_ref[...], ra_ref[...], rb_ref[...]

    def rope(t):
        return t * rc + pltpu.roll(t, LANES - 16, 1) * ra + pltpu.roll(t, 16, 1) * rb

    c_q = _rms(_dot(h, win_ref[:, 0:256]), gqa_ref[...]).astype(BF16)
    c_kv = _rms(_dot(h, win_ref[:, 256:384]), gkva_ref[...]).astype(BF16)
    hkr = _dot(h, win_ref[:, 384:512])
    kr_ms = jnp.sum(hkr * hkr, axis=-1, keepdims=True) * (1.0 / MLA_ROPE)
    k_pe = rope(hkr * lax.rsqrt(kr_ms + RMS_EPS) * gkr_ref[...])

    for pair in range(MLA_HEADS // 2):
        cols = slice(pair * 2 * LANES, (pair + 1) * 2 * LANES)
        q = _dot(c_q, wuq_ref[:, cols])
        q = q * lax.rsqrt(_segment_mean_sq(q, seg_mla_ref) + RMS_EPS) * gq_ref[:, cols]
        k = _dot(c_kv, wk_ref[:, cols])
        k = k * lax.rsqrt(_segment_mean_sq(k, seg_mla_ref) + RMS_EPS) * gk_ref[:, cols]
        for j in range(2):
            blk = slice(j * LANES, (j + 1) * LANES)
            dst = slice(pair * 2 * LANES + j * LANES, pair * 2 * LANES + (j + 1) * LANES)
            q_ref[:, dst] = rope(q[:, blk]).astype(BF16)
            k_ref[:, dst] = (k[:, blk] + k_pe).astype(BF16)
        v_ref[:, cols] = (_dot(c_kv, wv_ref[:, cols]) + vones_ref[:, cols]).astype(BF16)

    for pair in range(SWA_HEADS * SWA_HEAD_DIM // (2 * LANES)):
        cols = slice(pair * 2 * LANES, (pair + 1) * 2 * LANES)
        s_q = _dot(h, win_ref[:, 512 + pair * 2 * LANES:512 + (pair + 1) * 2 * LANES])
        s_q = s_q * lax.rsqrt(_segment_mean_sq(s_q, seg_swa_ref) + RMS_EPS) * gsq_ref[:, cols]
        sq_ref[:, cols] = s_q.astype(BF16)
    s_k = _dot(h, win_ref[:, 1024:1280])
    s_k = s_k * lax.rsqrt(_segment_mean_sq(s_k, seg_swa_ref) + RMS_EPS) * gsk_ref[...]
    sk_ref[...] = s_k.astype(BF16)
    sv_ref[...] = _dot(h, win_ref[:, 1280:1536]).astype(BF16)


def _proj_call(x1, batch, seq, consts):
    t, d = x1.shape
    assert seq % PROJ_ROWS == 0
    steps = seq // PROJ_ROWS
    row = lambda j, b: (b * steps + j, 0)
    pos = lambda j, b: (j, 0)
    in_specs = [pl.BlockSpec((PROJ_ROWS, d), row)]
    in_specs += [_const_spec(c.shape) for c in consts[:-3]]
    in_specs += [pl.BlockSpec((PROJ_ROWS, LANES), pos)] * 3
    widths = (MLA_HEADS * LANES, MLA_HEADS * LANES, MLA_HEADS * LANES,
              SWA_HEADS * SWA_HEAD_DIM, 2 * LANES, 2 * LANES)
    out_shape = [jax.ShapeDtypeStruct((t, w), BF16) for w in widths]
    out_specs = [pl.BlockSpec((PROJ_ROWS, w), row) for w in widths]
    block_bytes = (_nbytes((PROJ_ROWS, d), F32) + sum(_nbytes(c.shape, c.dtype) for c in consts[:-3])
                   + 3 * _nbytes((PROJ_ROWS, LANES), F32)
                   + sum(_nbytes((PROJ_ROWS, w), BF16) for w in widths))
    return pl.pallas_call(
        _proj_kernel,
        out_shape=out_shape,
        grid=(steps, batch),
        in_specs=in_specs,
        out_specs=out_specs,
        compiler_params=pltpu.CompilerParams(
            dimension_semantics=("parallel", "parallel"),
            vmem_limit_bytes=_vmem_limit(block_bytes)),
        name="proj",
    )(x1, *consts)


def _mla_kernel(q_ref, k_ref, v_ref, o_ref):
    lane = lax.broadcasted_iota(jnp.int32, o_ref.shape, 1)
    heads = []
    for j in range(2):
        blk = slice(j * LANES, (j + 1) * LANES)
        s = _dot_nt(q_ref[:, blk], k_ref[:, blk])
        m = jnp.max(s, axis=-1, keepdims=True)
        p = jnp.exp(s - m).astype(BF16)
        o = _dot(p, v_ref[:, blk])
        heads.append(o / pltpu.roll(o, LANES // 2, 1))
    o_ref[...] = jnp.where(lane < LANES // 2, heads[0], heads[1]).astype(BF16)


def _mla_call(q, k, v, batch, seq):
    t = q.shape[0]
    assert seq % MLA_Q_ROWS == 0
    steps = seq // MLA_Q_ROWS
    pairs = MLA_HEADS // 2
    q_spec = pl.BlockSpec((MLA_Q_ROWS, 2 * LANES), lambda b, hp, i: (b * steps + i, hp))
    kv_spec = pl.BlockSpec((seq, 2 * LANES), lambda b, hp, i: (b, hp))
    o_spec = pl.BlockSpec((MLA_Q_ROWS, LANES), lambda b, hp, i: (b * steps + i, hp))
    block_bytes = (2 * _nbytes((seq, 2 * LANES), BF16) + _nbytes((MLA_Q_ROWS, 3 * LANES), BF16)
                   + 2 * _nbytes((MLA_Q_ROWS, seq), F32))
    return pl.pallas_call(
        _mla_kernel,
        out_shape=jax.ShapeDtypeStruct((t, MLA_HEADS * MLA_V), BF16),
        grid=(batch, pairs, steps),
        in_specs=[q_spec, kv_spec, kv_spec],
        out_specs=o_spec,
        compiler_params=pltpu.CompilerParams(
            dimension_semantics=("parallel", "parallel", "parallel"),
            vmem_limit_bytes=_vmem_limit(block_bytes)),
        name="mla_attn",
    )(q, k, v)


def _swa_kernel(sink_ref, q_ref, k_ref, v_ref, o_ref):
    seq = k_ref.shape[0]
    i = pl.program_id(1)
    key0 = pl.multiple_of(jnp.clip(i * SWA_Q_ROWS - WINDOW, 0, seq - SWA_KEYS), LANES)
    kwin = k_ref[pl.ds(key0, SWA_KEYS), :]
    vwin = v_ref[pl.ds(key0, SWA_KEYS), :]

    shape = (SWA_Q_ROWS, SWA_KEYS)
    rel = (lax.broadcasted_iota(jnp.int32, shape, 1) - lax.broadcasted_iota(jnp.int32, shape, 0)
           + (key0 - i * SWA_Q_ROWS))
    dist = jnp.abs(rel).astype(F32)
    valid = dist <= float(WINDOW)
    lane = lax.broadcasted_iota(jnp.int32, (SWA_Q_ROWS, LANES), 1)
    low_half = lane < LANES // 2

    for g in range(SWA_KV_HEADS):
        kg = kwin[:, g * LANES:(g + 1) * LANES]
        vg = vwin[:, g * LANES:(g + 1) * LANES]
        for pair in range(SWA_GROUP // 2):
            blk_idx = g * (SWA_GROUP // 2) + pair
            q_blk = q_ref[:, blk_idx * LANES:(blk_idx + 1) * LANES].astype(F32)
            outs = []
            for half in range(2):
                head = 2 * blk_idx + half
                keep = low_half if half == 0 else jnp.logical_not(low_half)
                q_h = jnp.where(keep, q_blk, 0.0).astype(BF16)
                slope = 2.0 ** (-(8.0 / SWA_HEADS) * (head + 1))
                logits = jnp.where(valid, _dot_nt(q_h, kg) - slope * dist, -jnp.inf)
                sink = sink_ref[head]
                m = jnp.maximum(jnp.max(logits, axis=-1, keepdims=True), sink)
                p = jnp.exp(logits - m)
                denom = jnp.sum(p, axis=-1, keepdims=True) + jnp.exp(sink - m)
                outs.append(_dot(p.astype(BF16), vg) / denom)
            o_ref[:, blk_idx * LANES:(blk_idx + 1) * LANES] = (
                jnp.where(low_half, outs[0], outs[1]).astype(BF16))


def _swa_call(sink, sq, sk, sv, batch, seq):
    t, width = sq.shape
    assert seq % SWA_Q_ROWS == 0 and seq >= SWA_KEYS
    steps = seq // SWA_Q_ROWS
    q_spec = pl.BlockSpec((SWA_Q_ROWS, width), lambda b, i: (b * steps + i, 0))
    kv_spec = pl.BlockSpec((seq, 2 * LANES), lambda b, i: (b, 0))
    block_bytes = (2 * _nbytes((seq, 2 * LANES), BF16) + 2 * _nbytes((SWA_Q_ROWS, width), BF16)
                   + 4 * _nbytes((SWA_Q_ROWS, SWA_KEYS), F32))
    return pl.pallas_call(
        _swa_kernel,
        out_shape=jax.ShapeDtypeStruct((t, width), BF16),
        grid=(batch, steps),
        in_specs=[pl.BlockSpec(memory_space=pltpu.SMEM), q_spec, kv_spec, kv_spec],
        out_specs=q_spec,
        compiler_params=pltpu.CompilerParams(
            dimension_semantics=("parallel", "parallel"),
            vmem_limit_bytes=_vmem_limit(block_bytes)),
        name="swa_attn",
    )(sink, sq, sk, sv)


def _head_blocks(w, heads, width, lo, hi, at=0):
    rows = w.shape[0]
    part = w.reshape(rows, heads, width)[:, :, lo:hi]
    part = jnp.pad(part, ((0, 0), (0, 0), (at, LANES - at - (hi - lo))))
    return part.reshape(rows, heads * LANES)


def _proj_constants(seq, g_mix, w_in, g_q_a, w_uq, g_kv_a, w_ukv,
                    g_qn, g_qr, g_kn, g_kr, g_swa_q, g_swa_k):
    d = w_in.shape[0]
    q_rank, kv_rank = g_q_a.shape[0], g_kv_a.shape[0]
    o_kr = q_rank + kv_rank
    o_sq = o_kr + MLA_ROPE
    o_sk = o_sq + SWA_HEADS * SWA_HEAD_DIM
    o_sv = o_sk + SWA_KV_HEADS * SWA_HEAD_DIM
    kr_block = jnp.pad(w_in[:, o_kr:o_sq], ((0, 0), (MLA_NOPE, LANES - MLA_NOPE - MLA_ROPE)))
    dup = lambda w: jnp.repeat(w.reshape(d, SWA_KV_HEADS, 1, SWA_HEAD_DIM), 2, axis=2).reshape(d, -1)
    win = jnp.concatenate(
        [w_in[:, :o_kr], kr_block, w_in[:, o_sq:o_sk], dup(w_in[:, o_sk:o_sv]), dup(w_in[:, o_sv:])],
        axis=1).astype(BF16)

    qk_width = MLA_NOPE + MLA_ROPE
    wuq = _head_blocks(w_uq, MLA_HEADS, qk_width, 0, qk_width).astype(BF16)
    kv_width = MLA_NOPE + MLA_V
    wk = _head_blocks(w_ukv, MLA_HEADS, kv_width, 0, MLA_NOPE).astype(BF16)
    wv_even = _head_blocks(w_ukv, MLA_HEADS, kv_width, MLA_NOPE, kv_width, at=0)
    wv_odd = _head_blocks(w_ukv, MLA_HEADS, kv_width, MLA_NOPE, kv_width, at=LANES // 2)
    head_of_col = jnp.arange(MLA_HEADS * LANES) // LANES
    odd_col = (head_of_col % 2 == 1)[None, :]
    wv = jnp.where(odd_col, wv_odd, wv_even).astype(BF16)
    lane = jnp.arange(MLA_HEADS * LANES) % LANES
    vones = ((lane >= LANES // 2) != odd_col[0]).astype(F32)[None, :]

    scale = 1.0 / math.sqrt(qk_width)
    zeros = lambda n: jnp.zeros((n,), F32)
    gq = jnp.tile(jnp.concatenate([g_qn, g_qr, zeros(LANES - qk_width)]) * scale, MLA_HEADS)[None, :]
    gk = jnp.tile(jnp.concatenate([g_kn, zeros(LANES - MLA_NOPE)]), MLA_HEADS)[None, :]
    gkr = jnp.concatenate([zeros(MLA_NOPE), g_kr, zeros(LANES - qk_width)])[None, :]
    gsq = jnp.tile(g_swa_q * (1.0 / math.sqrt(SWA_HEAD_DIM)), SWA_HEADS)[None, :]
    gsk = jnp.tile(g_swa_k, 2 * SWA_KV_HEADS)[None, :]

    idx = jnp.arange(2 * LANES)
    in_blk = idx % LANES
    seg_id = jnp.where(in_blk < MLA_NOPE, 0, jnp.where(in_blk < qk_width, 1, 2)) + 3 * (idx // LANES)
    seg_len = jnp.where(in_blk < MLA_NOPE, MLA_NOPE, MLA_ROPE).astype(F32)
    same = (seg_id[:, None] == seg_id[None, :]) & (in_blk < qk_width)[:, None]
    seg_mla = jnp.where(same, 1.0 / seg_len[None, :], 0.0).astype(BF16)
    seg_swa = jnp.where((idx // SWA_HEAD_DIM)[:, None] == (idx // SWA_HEAD_DIM)[None, :],
                        1.0 / SWA_HEAD_DIM, 0.0).astype(BF16)

    pos = jnp.arange(seq, dtype=F32)
    inv = 1.0 / (ROPE_THETA ** (jnp.arange(0, MLA_ROPE, 2, dtype=F32) / MLA_ROPE))
    ang = pos[:, None] * inv[None, :]
    cos, sin = jnp.cos(ang), jnp.sin(ang)
    half = MLA_ROPE // 2
    pad_lo = jnp.ones((seq, MLA_NOPE), F32)
    pad_hi = jnp.ones((seq, LANES - qk_width), F32)
    rc = jnp.concatenate([pad_lo, cos, cos, pad_hi], axis=1)
    ra = jnp.concatenate([0 * pad_lo, -sin, jnp.zeros((seq, half), F32), 0 * pad_hi], axis=1)
    rb = jnp.concatenate([0 * pad_lo, jnp.zeros((seq, half), F32), sin, 0 * pad_hi], axis=1)

    row = lambda g: g.astype(F32)[None, :]
    return (row(g_mix), win, row(g_q_a), wuq, row(g_kv_a), wk, wv, vones,
            gq, gk, gkr, gsq, gsk, seg_mla, seg_swa, rc, ra, rb)


def kernel(x, g_ffn1, w1_gate, w1_up, w1_down, g_mix, w_in, g_q_a, w_uq, g_kv_a, w_ukv,
           g_mla_qn, g_mla_qr, g_mla_kn, g_mla_kr, g_swa_q, g_swa_k, sink, w_o,
           g_ffn2, w2_gate, w2_up, w2_down):
    batch, seq, d = x.shape
    x2d = x.reshape(batch * seq, d)
    bf = lambda w: w.astype(BF16)
    row = lambda g: g.astype(F32)[None, :]

    x1 = _ffn_call(x2d, row(g_ffn1), bf(w1_gate), bf(w1_up), bf(w1_down))
    consts = _proj_constants(seq, g_mix, w_in, g_q_a, w_uq, g_kv_a, w_ukv,
                             g_mla_qn, g_mla_qr, g_mla_kn, g_mla_kr, g_swa_q, g_swa_k)
    q, k, v, sq, sk, sv = _proj_call(x1, batch, seq, consts)
    y_a = _mla_call(q, k, v, batch, seq)
    y_b = _swa_call(sink.astype(F32), sq, sk, sv, batch, seq)
    out = _ffn_call(x1, row(g_ffn2), bf(w2_gate), bf(w2_up), bf(w2_down),
                    mix=(y_a, y_b, bf(w_o)))
    return out.reshape(batch, seq, d)
```

```python
import math

import jax
import jax.numpy as jnp
from jax import lax
from jax.experimental import pallas as pl
from jax.experimental.pallas import tpu as pltpu

F32 = jnp.float32
BF16 = jnp.bfloat16

RMS_EPS = 1e-6
MLA_HEADS = 8
MLA_NOPE = 64
MLA_ROPE = 32
MLA_V = 64
ROPE_THETA = 10000.0
SWA_HEADS = 8
SWA_KV_HEADS = 2
SWA_HEAD_DIM = 64
SWA_GROUP = SWA_HEADS // SWA_KV_HEADS
WINDOW = 128

LANES = 128
V7X_VMEM_BYTES = 64 * 1024 * 1024
VMEM_HEADROOM_BYTES = 8 * 1024 * 1024

FFN_ROWS = 512
FFN_CHUNK = 256
PROJ_ROWS = 512
MLA_Q_ROWS = 512
SWA_Q_ROWS = 128
SWA_KEYS = SWA_Q_ROWS + 2 * WINDOW


def _vmem_limit(block_bytes):
    want = 2 * block_bytes + 16 * 1024 * 1024
    return int(min(want, V7X_VMEM_BYTES - VMEM_HEADROOM_BYTES))


def _nbytes(shape, dtype):
    return math.prod(shape) * jnp.dtype(dtype).itemsize


def _const_spec(shape):
    zeros = (0,) * len(shape)
    return pl.BlockSpec(shape, lambda *_: zeros)


def _rms(x, gain):
    ms = jnp.mean(x * x, axis=-1, keepdims=True)
    return x * lax.rsqrt(ms + RMS_EPS) * gain


def _dot(a, b):
    return jnp.dot(a, b, preferred_element_type=F32)


def _dot_nt(a, b):
    return lax.dot_general(a, b, (((1,), (1,)), ((), ())), preferred_element_type=F32)


def _swiglu_residual(x, gain_ref, wg_ref, wu_ref, wd_ref, act_ref):
    h = _rms(x, gain_ref[...]).astype(BF16)
    d_ff = wg_ref.shape[1]
    for c in range(d_ff // FFN_CHUNK):
        cols = slice(c * FFN_CHUNK, (c + 1) * FFN_CHUNK)
        gate = _dot(h, wg_ref[:, cols])
        up = _dot(h, wu_ref[:, cols])
        act_ref[:, cols] = (gate / (1.0 + jnp.exp(-gate)) * up).astype(BF16)
    return x + 0.5 * _dot(act_ref[...], wd_ref[...])


def _ffn_kernel(x_ref, gain_ref, wg_ref, wu_ref, wd_ref, o_ref, act_ref):
    o_ref[...] = _swiglu_residual(x_ref[...], gain_ref, wg_ref, wu_ref, wd_ref, act_ref)


def _mix_ffn_kernel(x_ref, ya_ref, yb_ref, wo_ref, gain_ref, wg_ref, wu_ref, wd_ref,
                    o_ref, act_ref):
    half = ya_ref.shape[1]
    x2 = (x_ref[...] + _dot(ya_ref[...], wo_ref[:half, :])
          + _dot(yb_ref[...], wo_ref[half:, :]))
    o_ref[...] = _swiglu_residual(x2, gain_ref, wg_ref, wu_ref, wd_ref, act_ref)


def _ffn_call(x, gain, wg, wu, wd, mix=None):
    t, d = x.shape
    d_ff = wg.shape[1]
    assert t % FFN_ROWS == 0 and d_ff % FFN_CHUNK == 0
    row = lambda i: (i, 0)
    x_spec = pl.BlockSpec((FFN_ROWS, d), row)
    w_specs = [_const_spec((1, d)), _const_spec((d, d_ff)), _const_spec((d, d_ff)),
               _const_spec((d_ff, d))]
    block_bytes = (2 * _nbytes((FFN_ROWS, d), F32) + 3 * _nbytes((d, d_ff), BF16)
                   + _nbytes((FFN_ROWS, d_ff), BF16))
    if mix is None:
        body, operands, in_specs = _ffn_kernel, (x, gain, wg, wu, wd), [x_spec] + w_specs
    else:
        ya, yb, wo = mix
        half = ya.shape[1]
        y_spec = pl.BlockSpec((FFN_ROWS, half), row)
        body = _mix_ffn_kernel
        operands = (x, ya, yb, wo, gain, wg, wu, wd)
        in_specs = [x_spec, y_spec, y_spec, _const_spec(wo.shape)] + w_specs
        block_bytes += 2 * _nbytes((FFN_ROWS, half), BF16) + _nbytes(wo.shape, BF16)
    return pl.pallas_call(
        body,
        out_shape=jax.ShapeDtypeStruct((t, d), F32),
        grid=(t // FFN_ROWS,),
        in_specs=in_specs,
        out_specs=x_spec,
        scratch_shapes=[pltpu.VMEM((FFN_ROWS, d_ff), BF16)],
        compiler_params=pltpu.CompilerParams(
            dimension_semantics=("parallel",), vmem_limit_bytes=_vmem_limit(block_bytes)),
        name="ffn" if mix is None else "mix_ffn",
    )(*operands)


def _segment_mean_sq(x, seg_ref):
    sq = x * x
    hi = sq.astype(BF16)
    lo = (sq - hi.astype(F32)).astype(BF16)
    seg = seg_ref[...]
    return _dot(hi, seg) + _dot(lo, seg)


def _proj_kernel(x_ref, gmix_ref, win_ref, gqa_ref, wuq_ref, gkva_ref, wk_ref, wv_ref,
                 vones_ref, gq_ref, gk_ref, gkr_ref, gsq_ref, gsk_ref,
                 seg_mla_ref, seg_swa_ref, rc_ref, ra_ref, rb_ref,
                 q_ref, k_ref, v_ref, sq_ref, sk_ref, sv_ref):
    h = _rms(x_ref[...], gmix_ref[...]).astype(BF16)
    rc, ra, rb = rc_ref[...], ra_ref[...], rb_ref[...]

    def rope(t):
        return t * rc + pltpu.roll(t, LANES - 16, 1) * ra + pltpu.roll(t, 16, 1) * rb

    c_q = _rms(_dot(h, win_ref[:, 0:256]), gqa_ref[...]).astype(BF16)
    c_kv = _rms(_dot(h, win_ref[:, 256:384]), gkva_ref[...]).astype(BF16)
    hkr = _dot(h, win_ref[:, 384:512])
    kr_ms = jnp.sum(hkr * hkr, axis=-1, keepdims=True) * (1.0 / MLA_ROPE)
    k_pe = rope(hkr * lax.rsqrt(kr_ms + RMS_EPS) * gkr_ref[...])

    for pair in range(MLA_HEADS // 2):
        cols = slice(pair * 2 * LANES, (pair + 1) * 2 * LANES)
        q = _dot(c_q, wuq_ref[:, cols])
        q = q * lax.rsqrt(_segment_mean_sq(q, seg_mla_ref) + RMS_EPS) * gq_ref[:, cols]
        k = _dot(c_kv, wk_ref[:, cols])
        k = k * lax.rsqrt(_segment_mean_sq(k, seg_mla_ref) + RMS_EPS) * gk_ref[:, cols]
        for j in range(2):
            blk = slice(j * LANES, (j + 1) * LANES)
            dst = slice(pair * 2 * LANES + j * LANES, pair * 2 * LANES + (j + 1) * LANES)
            q_ref[:, dst] = rope(q[:, blk]).astype(BF16)
            k_ref[:, dst] = (k[:, blk] + k_pe).astype(BF16)
        v_ref[:, cols] = (_dot(c_kv, wv_ref[:, cols]) + vones_ref[:, cols]).astype(BF16)

    for pair in range(SWA_HEADS * SWA_HEAD_DIM // (2 * LANES)):
        cols = slice(pair * 2 * LANES, (pair + 1) * 2 * LANES)
        s_q = _dot(h, win_ref[:, 512 + pair * 2 * LANES:512 + (pair + 1) * 2 * LANES])
        s_q = s_q * lax.rsqrt(_segment_mean_sq(s_q, seg_swa_ref) + RMS_EPS) * gsq_ref[:, cols]
        sq_ref[:, cols] = s_q.astype(BF16)
    s_k = _dot(h, win_ref[:, 1024:1280])
    s_k = s_k * lax.rsqrt(_segment_mean_sq(s_k, seg_swa_ref) + RMS_EPS) * gsk_ref[...]
    sk_ref[...] = s_k.astype(BF16)
    sv_ref[...] = _dot(h, win_ref[:, 1280:1536]).astype(BF16)


def _proj_call(x1, batch, seq, consts):
    t, d = x1.shape
    assert seq % PROJ_ROWS == 0
    steps = seq // PROJ_ROWS
    row = lambda j, b: (b * steps + j, 0)
    pos = lambda j, b: (j, 0)
    in_specs = [pl.BlockSpec((PROJ_ROWS, d), row)]
    in_specs += [_const_spec(c.shape) for c in consts[:-3]]
    in_specs += [pl.BlockSpec((PROJ_ROWS, LANES), pos)] * 3
    widths = (MLA_HEADS * LANES, MLA_HEADS * LANES, MLA_HEADS * LANES,
              SWA_HEADS * SWA_HEAD_DIM, 2 * LANES, 2 * LANES)
    out_shape = [jax.ShapeDtypeStruct((t, w), BF16) for w in widths]
    out_specs = [pl.BlockSpec((PROJ_ROWS, w), row) for w in widths]
    block_bytes = (_nbytes((PROJ_ROWS, d), F32) + sum(_nbytes(c.shape, c.dtype) for c in consts[:-3])
                   + 3 * _nbytes((PROJ_ROWS, LANES), F32)
                   + sum(_nbytes((PROJ_ROWS, w), BF16) for w in widths))
    return pl.pallas_call(
        _proj_kernel,
        out_shape=out_shape,
        grid=(steps, batch),
        in_specs=in_specs,
        out_specs=out_specs,
        compiler_params=pltpu.CompilerParams(
            dimension_semantics=("parallel", "parallel"),
            vmem_limit_bytes=_vmem_limit(block_bytes)),
        name="proj",
    )(x1, *consts)


def _mla_kernel(q_ref, k_ref, v_ref, o_ref):
    lane = lax.broadcasted_iota(jnp.int32, (q_ref.shape[0], LANES), 1)

    def scores(h):
        blk = slice(h * LANES, (h + 1) * LANES)
        return _dot_nt(q_ref[:, blk], k_ref[:, blk])

    ahead = 1
    queue = [scores(h) for h in range(ahead)]
    outs = []
    for h in range(MLA_HEADS):
        s = queue.pop(0)
        if h + ahead < MLA_HEADS:
            queue.append(scores(h + ahead))
        m = jnp.max(s, axis=-1, keepdims=True)
        p = jnp.exp(s - m).astype(BF16)
        o = _dot(p, v_ref[:, h * LANES:(h + 1) * LANES])
        outs.append(o / pltpu.roll(o, LANES // 2, 1))
        if h % 2 == 1:
            o_ref[:, (h // 2) * LANES:(h // 2 + 1) * LANES] = (
                jnp.where(lane < LANES // 2, outs[h - 1], outs[h]).astype(BF16))


def _mla_call(q, k, v, batch, seq):
    t, width = q.shape
    assert seq % MLA_Q_ROWS == 0
    steps = seq // MLA_Q_ROWS
    q_spec = pl.BlockSpec((MLA_Q_ROWS, width), lambda b, i: (b * steps + i, 0))
    kv_spec = pl.BlockSpec((seq, width), lambda b, i: (b, 0))
    o_spec = pl.BlockSpec((MLA_Q_ROWS, MLA_HEADS * MLA_V), lambda b, i: (b * steps + i, 0))
    block_bytes = (2 * _nbytes((seq, width), BF16) + 2 * _nbytes((MLA_Q_ROWS, width), BF16)
                   + 4 * _nbytes((MLA_Q_ROWS, seq), F32))
    return pl.pallas_call(
        _mla_kernel,
        out_shape=jax.ShapeDtypeStruct((t, MLA_HEADS * MLA_V), BF16),
        grid=(batch, steps),
        in_specs=[q_spec, kv_spec, kv_spec],
        out_specs=o_spec,
        compiler_params=pltpu.CompilerParams(
            dimension_semantics=("parallel", "parallel"),
            vmem_limit_bytes=_vmem_limit(block_bytes)),
        name="mla_attn",
    )(q, k, v)


def _swa_kernel(sink_ref, q_ref, k_ref, v_ref, o_ref):
    seq = k_ref.shape[0]
    i = pl.program_id(1)
    key0 = pl.multiple_of(jnp.clip(i * SWA_Q_ROWS - WINDOW, 0, seq - SWA_KEYS), LANES)
    kwin = k_ref[pl.ds(key0, SWA_KEYS), :]
    vwin = v_ref[pl.ds(key0, SWA_KEYS), :]

    shape = (SWA_Q_ROWS, SWA_KEYS)
    rel = (lax.broadcasted_iota(jnp.int32, shape, 1) - lax.broadcasted_iota(jnp.int32, shape, 0)
           + (key0 - i * SWA_Q_ROWS))
    dist = jnp.abs(rel).astype(F32)
    valid = dist <= float(WINDOW)
    lane = lax.broadcasted_iota(jnp.int32, (SWA_Q_ROWS, LANES), 1)
    low_half = lane < LANES // 2

    rows = SWA_Q_ROWS
    for g in range(SWA_KV_HEADS):
        kg = kwin[:, g * LANES:(g + 1) * LANES]
        vg = vwin[:, g * LANES:(g + 1) * LANES]
        heads = [g * SWA_GROUP + a for a in range(SWA_GROUP)]
        q_rows = []
        for head in heads:
            q_blk = q_ref[:, (head // 2) * LANES:(head // 2 + 1) * LANES].astype(F32)
            keep = low_half if head % 2 == 0 else jnp.logical_not(low_half)
            q_rows.append(jnp.where(keep, q_blk, 0.0).astype(BF16))
        scores = _dot_nt(jnp.concatenate(q_rows, axis=0), kg)
        p_rows, denoms = [], []
        for a, head in enumerate(heads):
            slope = 2.0 ** (-(8.0 / SWA_HEADS) * (head + 1))
            logits = jnp.where(valid, scores[a * rows:(a + 1) * rows] - slope * dist, -jnp.inf)
            sink = sink_ref[head]
            m = jnp.maximum(jnp.max(logits, axis=-1, keepdims=True), sink)
            p = jnp.exp(logits - m)
            denoms.append(jnp.sum(p, axis=-1, keepdims=True) + jnp.exp(sink - m))
            p_rows.append(p.astype(BF16))
        o = _dot(jnp.concatenate(p_rows, axis=0), vg)
        outs = [o[a * rows:(a + 1) * rows] / denoms[a] for a in range(SWA_GROUP)]
        for pair in range(SWA_GROUP // 2):
            blk_idx = g * (SWA_GROUP // 2) + pair
            o_ref[:, blk_idx * LANES:(blk_idx + 1) * LANES] = (
                jnp.where(low_half, outs[2 * pair], outs[2 * pair + 1]).astype(BF16))


def _swa_call(sink, sq, sk, sv, batch, seq):
    t, width = sq.shape
    assert seq % SWA_Q_ROWS == 0 and seq >= SWA_KEYS
    steps = seq // SWA_Q_ROWS
    q_spec = pl.BlockSpec((SWA_Q_ROWS, width), lambda b, i: (b * steps + i, 0))
    kv_spec = pl.BlockSpec((seq, 2 * LANES), lambda b, i: (b, 0))
    block_bytes = (2 * _nbytes((seq, 2 * LANES), BF16) + 2 * _nbytes((SWA_Q_ROWS, width), BF16)
                   + 4 * _nbytes((SWA_Q_ROWS, SWA_KEYS), F32))
    return pl.pallas_call(
        _swa_kernel,
        out_shape=jax.ShapeDtypeStruct((t, width), BF16),
        grid=(batch, steps),
        in_specs=[pl.BlockSpec(memory_space=pltpu.SMEM), q_spec, kv_spec, kv_spec],
        out_specs=q_spec,
        compiler_params=pltpu.CompilerParams(
            dimension_semantics=("parallel", "parallel"),
            vmem_limit_bytes=_vmem_limit(block_bytes)),
        name="swa_attn",
    )(sink, sq, sk, sv)


def _head_blocks(w, heads, width, lo, hi, at=0):
    rows = w.shape[0]
    part = w.reshape(rows, heads, width)[:, :, lo:hi]
    part = jnp.pad(part, ((0, 0), (0, 0), (at, LANES - at - (hi - lo))))
    return part.reshape(rows, heads * LANES)


def _proj_constants(seq, g_mix, w_in, g_q_a, w_uq, g_kv_a, w_ukv,
                    g_qn, g_qr, g_kn, g_kr, g_swa_q, g_swa_k):
    d = w_in.shape[0]
    q_rank, kv_rank = g_q_a.shape[0], g_kv_a.shape[0]
    o_kr = q_rank + kv_rank
    o_sq = o_kr + MLA_ROPE
    o_sk = o_sq + SWA_HEADS * SWA_HEAD_DIM
    o_sv = o_sk + SWA_KV_HEADS * SWA_HEAD_DIM
    kr_block = jnp.pad(w_in[:, o_kr:o_sq], ((0, 0), (MLA_NOPE, LANES - MLA_NOPE - MLA_ROPE)))
    dup = lambda w: jnp.repeat(w.reshape(d, SWA_KV_HEADS, 1, SWA_HEAD_DIM), 2, axis=2).reshape(d, -1)
    win = jnp.concatenate(
        [w_in[:, :o_kr], kr_block, w_in[:, o_sq:o_sk], dup(w_in[:, o_sk:o_sv]), dup(w_in[:, o_sv:])],
        axis=1).astype(BF16)

    qk_width = MLA_NOPE + MLA_ROPE
    wuq = _head_blocks(w_uq, MLA_HEADS, qk_width, 0, qk_width).astype(BF16)
    kv_width = MLA_NOPE + MLA_V
    wk = _head_blocks(w_ukv, MLA_HEADS, kv_width, 0, MLA_NOPE).astype(BF16)
    wv_even = _head_blocks(w_ukv, MLA_HEADS, kv_width, MLA_NOPE, kv_width, at=0)
    wv_odd = _head_blocks(w_ukv, MLA_HEADS, kv_width, MLA_NOPE, kv_width, at=LANES // 2)
    head_of_col = jnp.arange(MLA_HEADS * LANES) // LANES
    odd_col = (head_of_col % 2 == 1)[None, :]
    wv = jnp.where(odd_col, wv_odd, wv_even).astype(BF16)
    lane = jnp.arange(MLA_HEADS * LANES) % LANES
    vones = ((lane >= LANES // 2) != odd_col[0]).astype(F32)[None, :]

    scale = 1.0 / math.sqrt(qk_width)
    zeros = lambda n: jnp.zeros((n,), F32)
    gq = jnp.tile(jnp.concatenate([g_qn, g_qr, zeros(LANES - qk_width)]) * scale, MLA_HEADS)[None, :]
    gk = jnp.tile(jnp.concatenate([g_kn, zeros(LANES - MLA_NOPE)]), MLA_HEADS)[None, :]
    gkr = jnp.concatenate([zeros(MLA_NOPE), g_kr, zeros(LANES - qk_width)])[None, :]
    gsq = jnp.tile(g_swa_q * (1.0 / math.sqrt(SWA_HEAD_DIM)), SWA_HEADS)[None, :]
    gsk = jnp.tile(g_swa_k, 2 * SWA_KV_HEADS)[None, :]

    idx = jnp.arange(2 * LANES)
    in_blk = idx % LANES
    seg_id = jnp.where(in_blk < MLA_NOPE, 0, jnp.where(in_blk < qk_width, 1, 2)) + 3 * (idx // LANES)
    seg_len = jnp.where(in_blk < MLA_NOPE, MLA_NOPE, MLA_ROPE).astype(F32)
    same = (seg_id[:, None] == seg_id[None, :]) & (in_blk < qk_width)[:, None]
    seg_mla = jnp.where(same, 1.0 / seg_len[None, :], 0.0).astype(BF16)
    seg_swa = jnp.where((idx // SWA_HEAD_DIM)[:, None] == (idx // SWA_HEAD_DIM)[None, :],
                        1.0 / SWA_HEAD_DIM, 0.0).astype(BF16)

    pos = jnp.arange(seq, dtype=F32)
    inv = 1.0 / (ROPE_THETA ** (jnp.arange(0, MLA_ROPE, 2, dtype=F32) / MLA_ROPE))
    ang = pos[:, None] * inv[None, :]
    cos, sin = jnp.cos(ang), jnp.sin(ang)
    half = MLA_ROPE // 2
    pad_lo = jnp.ones((seq, MLA_NOPE), F32)
    pad_hi = jnp.ones((seq, LANES - qk_width), F32)
    rc = jnp.concatenate([pad_lo, cos, cos, pad_hi], axis=1)
    ra = jnp.concatenate([0 * pad_lo, -sin, jnp.zeros((seq, half), F32), 0 * pad_hi], axis=1)
    rb = jnp.concatenate([0 * pad_lo, jnp.zeros((seq, half), F32), sin, 0 * pad_hi], axis=1)

    row = lambda g: g.astype(F32)[None, :]
    return (row(g_mix), win, row(g_q_a), wuq, row(g_kv_a), wk, wv, vones,
            gq, gk, gkr, gsq, gsk, seg_mla, seg_swa, rc, ra, rb)


def kernel(x, g_ffn1, w1_gate, w1_up, w1_down, g_mix, w_in, g_q_a, w_uq, g_kv_a, w_ukv,
           g_mla_qn, g_mla_qr, g_mla_kn, g_mla_kr, g_swa_q, g_swa_k, sink, w_o,
           g_ffn2, w2_gate, w2_up, w2_down):
    batch, seq, d = x.shape
    x2d = x.reshape(batch * seq, d)
    bf = lambda w: w.astype(BF16)
    row = lambda g: g.astype(F32)[None, :]

    x1 = _ffn_call(x2d, row(g_ffn1), bf(w1_gate), bf(w1_up), bf(w1_down))
    consts = _proj_constants(seq, g_mix, w_in, g_q_a, w_uq, g_kv_a, w_ukv,
                             g_mla_qn, g_mla_qr, g_mla_kn, g_mla_kr, g_swa_q, g_swa_k)
    q, k, v, sq, sk, sv = _proj_call(x1, batch, seq, consts)
    y_a = _mla_call(q, k, v, batch, seq)
    y_b = _swa_call(sink.astype(F32), sq, sk, sv, batch, seq)
    out = _ffn_call(x1, row(g_ffn2), bf(w2_gate), bf(w2_up), bf(w2_down),
                    mix=(y_a, y_b, bf(w_o)))
    return out.reshape(batch, seq, d)
```

```python
import math

import jax
import jax.numpy as jnp
from jax import lax
from jax.experimental import pallas as pl
from jax.experimental.pallas import tpu as pltpu

F32 = jnp.float32
BF16 = jnp.bfloat16

RMS_EPS = 1e-6
MLA_HEADS = 8
MLA_NOPE = 64
MLA_ROPE = 32
MLA_V = 64
ROPE_THETA = 10000.0
SWA_HEADS = 8
SWA_KV_HEADS = 2
SWA_HEAD_DIM = 64
SWA_GROUP = SWA_HEADS // SWA_KV_HEADS
WINDOW = 128
LOG2E = math.log2(math.e)

LANES = 128
V7X_VMEM_BYTES = 64 * 1024 * 1024
VMEM_HEADROOM_BYTES = 8 * 1024 * 1024

FFN_ROWS = 512
FFN_CHUNK = 256
PROJ_ROWS = 512
MLA_Q_ROWS = 512
SWA_Q_ROWS = 128
SWA_KEYS = SWA_Q_ROWS + 2 * WINDOW
SWA_SUB = 4


def _vmem_limit(block_bytes):
    want = 2 * block_bytes + 16 * 1024 * 1024
    return int(min(want, V7X_VMEM_BYTES - VMEM_HEADROOM_BYTES))


def _nbytes(shape, dtype):
    return math.prod(shape) * jnp.dtype(dtype).itemsize


def _const_spec(shape):
    zeros = (0,) * len(shape)
    return pl.BlockSpec(shape, lambda *_: zeros)


def _rms(x, gain):
    ms = jnp.mean(x * x, axis=-1, keepdims=True)
    return x * lax.rsqrt(ms + RMS_EPS) * gain


def _dot(a, b):
    return jnp.dot(a, b, preferred_element_type=F32)


def _dot_nt(a, b):
    return lax.dot_general(a, b, (((1,), (1,)), ((), ())), preferred_element_type=F32)


def _swiglu_residual(x, gain_ref, wg_ref, wu_ref, wd_ref, act_ref):
    h = _rms(x, gain_ref[...]).astype(BF16)
    d_ff = wg_ref.shape[1]
    for c in range(d_ff // FFN_CHUNK):
        cols = slice(c * FFN_CHUNK, (c + 1) * FFN_CHUNK)
        gate = _dot(h, wg_ref[:, cols])
        up = _dot(h, wu_ref[:, cols])
        act_ref[:, cols] = (gate / (1.0 + jnp.exp(-gate)) * up).astype(BF16)
    return x + 0.5 * _dot(act_ref[...], wd_ref[...])


def _ffn_kernel(x_ref, gain_ref, wg_ref, wu_ref, wd_ref, o_ref, act_ref):
    o_ref[...] = _swiglu_residual(x_ref[...], gain_ref, wg_ref, wu_ref, wd_ref, act_ref)


def _mix_ffn_kernel(x_ref, ya_ref, yb_ref, wo_ref, gain_ref, wg_ref, wu_ref, wd_ref,
                    o_ref, act_ref):
    half = ya_ref.shape[1]
    x2 = (x_ref[...] + _dot(ya_ref[...], wo_ref[:half, :])
          + _dot(yb_ref[...], wo_ref[half:, :]))
    o_ref[...] = _swiglu_residual(x2, gain_ref, wg_ref, wu_ref, wd_ref, act_ref)


def _ffn_call(x, gain, wg, wu, wd, mix=None):
    t, d = x.shape
    d_ff = wg.shape[1]
    assert t % FFN_ROWS == 0 and d_ff % FFN_CHUNK == 0
    row = lambda i: (i, 0)
    x_spec = pl.BlockSpec((FFN_ROWS, d), row)
    w_specs = [_const_spec((1, d)), _const_spec((d, d_ff)), _const_spec((d, d_ff)),
               _const_spec((d_ff, d))]
    block_bytes = (2 * _nbytes((FFN_ROWS, d), F32) + 3 * _nbytes((d, d_ff), BF16)
                   + _nbytes((FFN_ROWS, d_ff), BF16))
    if mix is None:
        body, operands, in_specs = _ffn_kernel, (x, gain, wg, wu, wd), [x_spec] + w_specs
    else:
        ya, yb, wo = mix
        half = ya.shape[1]
        y_spec = pl.BlockSpec((FFN_ROWS, half), row)
        body = _mix_ffn_kernel
        operands = (x, ya, yb, wo, gain, wg, wu, wd)
        in_specs = [x_spec, y_spec, y_spec, _const_spec(wo.shape)] + w_specs
        block_bytes += 2 * _nbytes((FFN_ROWS, half), BF16) + _nbytes(wo.shape, BF16)
    return pl.pallas_call(
        body,
        out_shape=jax.ShapeDtypeStruct((t, d), F32),
        grid=(t // FFN_ROWS,),
        in_specs=in_specs,
        out_specs=x_spec,
        scratch_shapes=[pltpu.VMEM((FFN_ROWS, d_ff), BF16)],
        compiler_params=pltpu.CompilerParams(
            dimension_semantics=("parallel",), vmem_limit_bytes=_vmem_limit(block_bytes)),
        name="ffn" if mix is None else "mix_ffn",
    )(*operands)


def _segment_mean_sq(x, seg_ref):
    return _dot((x * x).astype(BF16), seg_ref[...])


def _proj_kernel(x_ref, gmix_ref, win_ref, gqa_ref, wuq_ref, gkva_ref, wk_ref, wv_ref,
                 vones_ref, gq_ref, gk_ref, gkr_ref, gsq_ref, gsk_ref,
                 seg_mla_ref, seg_swa_ref, rc_ref, ra_ref, rb_ref,
                 q_ref, k_ref, v_ref, sq_ref, sk_ref, sv_ref):
    h = _rms(x_ref[...], gmix_ref[...]).astype(BF16)
    rc, ra, rb = rc_ref[...], ra_ref[...], rb_ref[...]

    def rope(t):
        return t * rc + pltpu.roll(t, LANES - 16, 1) * ra + pltpu.roll(t, 16, 1) * rb

    c_q = _rms(_dot(h, win_ref[:, 0:256]), gqa_ref[...]).astype(BF16)
    c_kv = _rms(_dot(h, win_ref[:, 256:384]), gkva_ref[...]).astype(BF16)
    hkr = _dot(h, win_ref[:, 384:512])
    kr_ms = jnp.sum(hkr * hkr, axis=-1, keepdims=True) * (1.0 / MLA_ROPE)
    k_pe = rope(hkr * lax.rsqrt(kr_ms + RMS_EPS) * gkr_ref[...])

    pairs = range(MLA_HEADS // 2)
    grp = lambda i: slice(i * 2 * LANES, (i + 1) * 2 * LANES)
    q_raw = [_dot(c_q, wuq_ref[:, grp(i)]) for i in pairs]
    k_raw = [_dot(c_kv, wk_ref[:, grp(i)]) for i in pairs]
    s_raw = [_dot(h, win_ref[:, 512 + i * 2 * LANES:512 + (i + 1) * 2 * LANES]) for i in range(3)]
    sv_ref[...] = _dot(h, win_ref[:, 1280:1536]).astype(BF16)
    for i in pairs:
        v_ref[:, grp(i)] = (_dot(c_kv, wv_ref[:, grp(i)]) + vones_ref[:, grp(i)]).astype(BF16)

    q_ms = [_segment_mean_sq(t, seg_mla_ref) for t in q_raw]
    k_ms = [_segment_mean_sq(t, seg_mla_ref) for t in k_raw]
    s_ms = [_segment_mean_sq(t, seg_swa_ref) for t in s_raw]

    for i in pairs:
        q = q_raw[i] * lax.rsqrt(q_ms[i] + RMS_EPS) * gq_ref[:, grp(i)]
        k = k_raw[i] * lax.rsqrt(k_ms[i] + RMS_EPS) * gk_ref[:, grp(i)]
        for j in range(2):
            blk = slice(j * LANES, (j + 1) * LANES)
            dst = slice((2 * i + j) * LANES, (2 * i + j + 1) * LANES)
            q_ref[:, dst] = rope(q[:, blk]).astype(BF16)
            k_ref[:, dst] = (k[:, blk] + k_pe).astype(BF16)
    for i in range(2):
        sq_ref[:, grp(i)] = (s_raw[i] * lax.rsqrt(s_ms[i] + RMS_EPS) * gsq_ref[:, grp(i)]).astype(BF16)
    sk_ref[...] = (s_raw[2] * lax.rsqrt(s_ms[2] + RMS_EPS) * gsk_ref[...]).astype(BF16)


def _proj_call(x1, batch, seq, consts):
    t, d = x1.shape
    assert seq % PROJ_ROWS == 0
    steps = seq // PROJ_ROWS
    row = lambda j, b: (b * steps + j, 0)
    pos = lambda j, b: (j, 0)
    in_specs = [pl.BlockSpec((PROJ_ROWS, d), row)]
    in_specs += [_const_spec(c.shape) for c in consts[:-3]]
    in_specs += [pl.BlockSpec((PROJ_ROWS, LANES), pos)] * 3
    widths = (MLA_HEADS * LANES, MLA_HEADS * LANES, MLA_HEADS * LANES,
              SWA_HEADS * SWA_HEAD_DIM, 2 * LANES, 2 * LANES)
    out_shape = [jax.ShapeDtypeStruct((t, w), BF16) for w in widths]
    out_specs = [pl.BlockSpec((PROJ_ROWS, w), row) for w in widths]
    block_bytes = (_nbytes((PROJ_ROWS, d), F32) + sum(_nbytes(c.shape, c.dtype) for c in consts[:-3])
                   + 3 * _nbytes((PROJ_ROWS, LANES), F32)
                   + sum(_nbytes((PROJ_ROWS, w), BF16) for w in widths))
    return pl.pallas_call(
        _proj_kernel,
        out_shape=out_shape,
        grid=(steps, batch),
        in_specs=in_specs,
        out_specs=out_specs,
        compiler_params=pltpu.CompilerParams(
            dimension_semantics=("parallel", "parallel"),
            vmem_limit_bytes=_vmem_limit(block_bytes)),
        name="proj",
    )(x1, *consts)


def _mla_kernel(q_ref, k_ref, v_ref, o_ref):
    lane = lax.broadcasted_iota(jnp.int32, (q_ref.shape[0], LANES), 1)

    def scores(h):
        blk = slice(h * LANES, (h + 1) * LANES)
        return _dot_nt(q_ref[:, blk], k_ref[:, blk])

    ahead = 1
    queue = [scores(h) for h in range(ahead)]
    outs = []
    for h in range(MLA_HEADS):
        s = queue.pop(0)
        if h + ahead < MLA_HEADS:
            queue.append(scores(h + ahead))
        m = jnp.max(s, axis=-1, keepdims=True)
        p = jnp.exp2(s - m).astype(BF16)
        o = _dot(p, v_ref[:, h * LANES:(h + 1) * LANES])
        outs.append(o / pltpu.roll(o, LANES // 2, 1))
        if h % 2 == 1:
            o_ref[:, (h // 2) * LANES:(h // 2 + 1) * LANES] = (
                jnp.where(lane < LANES // 2, outs[h - 1], outs[h]).astype(BF16))


def _mla_call(q, k, v, batch, seq):
    t, width = q.shape
    assert seq % MLA_Q_ROWS == 0
    steps = seq // MLA_Q_ROWS
    q_spec = pl.BlockSpec((MLA_Q_ROWS, width), lambda b, i: (b * steps + i, 0))
    kv_spec = pl.BlockSpec((seq, width), lambda b, i: (b, 0))
    o_spec = pl.BlockSpec((MLA_Q_ROWS, MLA_HEADS * MLA_V), lambda b, i: (b * steps + i, 0))
    block_bytes = (2 * _nbytes((seq, width), BF16) + 2 * _nbytes((MLA_Q_ROWS, width), BF16)
                   + 4 * _nbytes((MLA_Q_ROWS, seq), F32))
    return pl.pallas_call(
        _mla_kernel,
        out_shape=jax.ShapeDtypeStruct((t, MLA_HEADS * MLA_V), BF16),
        grid=(batch, steps),
        in_specs=[q_spec, kv_spec, kv_spec],
        out_specs=o_spec,
        compiler_params=pltpu.CompilerParams(
            dimension_semantics=("parallel", "parallel"),
            vmem_limit_bytes=_vmem_limit(block_bytes)),
        name="mla_attn",
    )(q, k, v)


def _swa_key_start(i, seq):
    return jnp.clip(i * SWA_Q_ROWS - WINDOW, 0, seq - SWA_KEYS)


def _swa_bias_tables(seq):
    steps = seq // SWA_Q_ROWS
    r = jnp.arange(SWA_Q_ROWS)[:, None]
    c = jnp.arange(SWA_KEYS)[None, :]
    slopes = 2.0 ** (-(8.0 / SWA_HEADS) * jnp.arange(1, SWA_HEADS + 1, dtype=F32))
    cases = []
    for i in (0, 1, steps - 1):
        dist = jnp.abs(c - r + (_swa_key_start(i, seq) - i * SWA_Q_ROWS)).astype(F32)
        bias = -LOG2E * slopes[:, None, None] * dist[None]
        cases.append(jnp.where((dist <= WINDOW)[None], bias, -jnp.inf))
    return jnp.stack(cases)


def _swa_kernel(sink_ref, bias_ref, q_ref, k_ref, v_ref, o_ref):
    seq = k_ref.shape[0]
    blocks = seq // SWA_Q_ROWS
    rows = SWA_Q_ROWS
    lane = lax.broadcasted_iota(jnp.int32, (rows, LANES), 1)
    low_half = lane < LANES // 2

    for u in range(SWA_SUB):
        blk = pl.program_id(1) * SWA_SUB + u
        key0 = pl.multiple_of(_swa_key_start(blk, seq), LANES)
        case = jnp.where(blk == 0, 0, jnp.where(blk == blocks - 1, 2, 1))
        qrows = slice(u * rows, (u + 1) * rows)
        kwin = k_ref[pl.ds(key0, SWA_KEYS), :]
        vwin = v_ref[pl.ds(key0, SWA_KEYS), :]
        for g in range(SWA_KV_HEADS):
            kg = kwin[:, g * LANES:(g + 1) * LANES]
            vg = vwin[:, g * LANES:(g + 1) * LANES]
            heads = [g * SWA_GROUP + a for a in range(SWA_GROUP)]
            q_rows = []
            for head in heads:
                q_blk = q_ref[qrows, (head // 2) * LANES:(head // 2 + 1) * LANES].astype(F32)
                keep = low_half if head % 2 == 0 else jnp.logical_not(low_half)
                q_rows.append(jnp.where(keep, q_blk, 0.0).astype(BF16))
            scores = _dot_nt(jnp.concatenate(q_rows, axis=0), kg)
            p_rows, denoms = [], []
            for a, head in enumerate(heads):
                logits = scores[a * rows:(a + 1) * rows] + bias_ref[case, head]
                sink = sink_ref[head] * LOG2E
                m = jnp.maximum(jnp.max(logits, axis=-1, keepdims=True), sink)
                p = jnp.exp2(logits - m)
                denoms.append(jnp.sum(p, axis=-1, keepdims=True) + jnp.exp2(sink - m))
                p_rows.append(p.astype(BF16))
            o = _dot(jnp.concatenate(p_rows, axis=0), vg)
            outs = [o[a * rows:(a + 1) * rows] / denoms[a] for a in range(SWA_GROUP)]
            for pair in range(SWA_GROUP // 2):
                blk_idx = g * (SWA_GROUP // 2) + pair
                o_ref[qrows, blk_idx * LANES:(blk_idx + 1) * LANES] = (
                    jnp.where(low_half, outs[2 * pair], outs[2 * pair + 1]).astype(BF16))


def _swa_call(sink, sq, sk, sv, batch, seq):
    t, width = sq.shape
    step_rows = SWA_SUB * SWA_Q_ROWS
    assert seq % step_rows == 0 and seq >= SWA_KEYS and seq // SWA_Q_ROWS >= 3
    steps = seq // step_rows
    q_spec = pl.BlockSpec((step_rows, width), lambda b, i: (b * steps + i, 0))
    kv_spec = pl.BlockSpec((seq, 2 * LANES), lambda b, i: (b, 0))
    bias = _swa_bias_tables(seq)
    block_bytes = (2 * _nbytes((seq, 2 * LANES), BF16) + 2 * _nbytes((step_rows, width), BF16)
                   + _nbytes(bias.shape, F32)
                   + 2 * SWA_SUB * SWA_GROUP * _nbytes((SWA_Q_ROWS, SWA_KEYS), F32))
    return pl.pallas_call(
        _swa_kernel,
        out_shape=jax.ShapeDtypeStruct((t, width), BF16),
        grid=(batch, steps),
        in_specs=[pl.BlockSpec(memory_space=pltpu.SMEM), _const_spec(bias.shape),
                  q_spec, kv_spec, kv_spec],
        out_specs=q_spec,
        compiler_params=pltpu.CompilerParams(
            dimension_semantics=("parallel", "parallel"),
            vmem_limit_bytes=_vmem_limit(block_bytes)),
        name="swa_attn",
    )(sink, bias, sq, sk, sv)


def _head_blocks(w, heads, width, lo, hi, at=0):
    rows = w.shape[0]
    part = w.reshape(rows, heads, width)[:, :, lo:hi]
    part = jnp.pad(part, ((0, 0), (0, 0), (at, LANES - at - (hi - lo))))
    return part.reshape(rows, heads * LANES)


def _proj_constants(seq, g_mix, w_in, g_q_a, w_uq, g_kv_a, w_ukv,
                    g_qn, g_qr, g_kn, g_kr, g_swa_q, g_swa_k):
    d = w_in.shape[0]
    q_rank, kv_rank = g_q_a.shape[0], g_kv_a.shape[0]
    o_kr = q_rank + kv_rank
    o_sq = o_kr + MLA_ROPE
    o_sk = o_sq + SWA_HEADS * SWA_HEAD_DIM
    o_sv = o_sk + SWA_KV_HEADS * SWA_HEAD_DIM
    kr_block = jnp.pad(w_in[:, o_kr:o_sq], ((0, 0), (MLA_NOPE, LANES - MLA_NOPE - MLA_ROPE)))
    dup = lambda w: jnp.repeat(w.reshape(d, SWA_KV_HEADS, 1, SWA_HEAD_DIM), 2, axis=2).reshape(d, -1)
    win = jnp.concatenate(
        [w_in[:, :o_kr], kr_block, w_in[:, o_sq:o_sk], dup(w_in[:, o_sk:o_sv]), dup(w_in[:, o_sv:])],
        axis=1).astype(BF16)

    qk_width = MLA_NOPE + MLA_ROPE
    wuq = _head_blocks(w_uq, MLA_HEADS, qk_width, 0, qk_width).astype(BF16)
    kv_width = MLA_NOPE + MLA_V
    wk = _head_blocks(w_ukv, MLA_HEADS, kv_width, 0, MLA_NOPE).astype(BF16)
    wv_even = _head_blocks(w_ukv, MLA_HEADS, kv_width, MLA_NOPE, kv_width, at=0)
    wv_odd = _head_blocks(w_ukv, MLA_HEADS, kv_width, MLA_NOPE, kv_width, at=LANES // 2)
    head_of_col = jnp.arange(MLA_HEADS * LANES) // LANES
    odd_col = (head_of_col % 2 == 1)[None, :]
    wv = jnp.where(odd_col, wv_odd, wv_even).astype(BF16)
    lane = jnp.arange(MLA_HEADS * LANES) % LANES
    vones = ((lane >= LANES // 2) != odd_col[0]).astype(F32)[None, :]

    scale = LOG2E / math.sqrt(qk_width)
    zeros = lambda n: jnp.zeros((n,), F32)
    gq = jnp.tile(jnp.concatenate([g_qn, g_qr, zeros(LANES - qk_width)]) * scale, MLA_HEADS)[None, :]
    gk = jnp.tile(jnp.concatenate([g_kn, zeros(LANES - MLA_NOPE)]), MLA_HEADS)[None, :]
    gkr = jnp.concatenate([zeros(MLA_NOPE), g_kr, zeros(LANES - qk_width)])[None, :]
    gsq = jnp.tile(g_swa_q * (LOG2E / math.sqrt(SWA_HEAD_DIM)), SWA_HEADS)[None, :]
    gsk = jnp.tile(g_swa_k, 2 * SWA_KV_HEADS)[None, :]

    idx = jnp.arange(2 * LANES)
    in_blk = idx % LANES
    seg_id = jnp.where(in_blk < MLA_NOPE, 0, jnp.where(in_blk < qk_width, 1, 2)) + 3 * (idx // LANES)
    seg_len = jnp.where(in_blk < MLA_NOPE, MLA_NOPE, MLA_ROPE).astype(F32)
    same = (seg_id[:, None] == seg_id[None, :]) & (in_blk < qk_width)[:, None]
    seg_mla = jnp.where(same, 1.0 / seg_len[None, :], 0.0).astype(BF16)
    seg_swa = jnp.where((idx // SWA_HEAD_DIM)[:, None] == (idx // SWA_HEAD_DIM)[None, :],
                        1.0 / SWA_HEAD_DIM, 0.0).astype(BF16)

    pos = jnp.arange(seq, dtype=F32)
    inv = 1.0 / (ROPE_THETA ** (jnp.arange(0, MLA_ROPE, 2, dtype=F32) / MLA_ROPE))
    ang = pos[:, None] * inv[None, :]
    cos, sin = jnp.cos(ang), jnp.sin(ang)
    half = MLA_ROPE // 2
    pad_lo = jnp.ones((seq, MLA_NOPE), F32)
    pad_hi = jnp.ones((seq, LANES - qk_width), F32)
    rc = jnp.concatenate([pad_lo, cos, cos, pad_hi], axis=1)
    ra = jnp.concatenate([0 * pad_lo, -sin, jnp.zeros((seq, half), F32), 0 * pad_hi], axis=1)
    rb = jnp.concatenate([0 * pad_lo, jnp.zeros((seq, half), F32), sin, 0 * pad_hi], axis=1)

    row = lambda g: g.astype(F32)[None, :]
    return (row(g_mix), win, row(g_q_a), wuq, row(g_kv_a), wk, wv, vones,
            gq, gk, gkr, gsq, gsk, seg_mla, seg_swa, rc, ra, rb)


def kernel(x, g_ffn1, w1_gate, w1_up, w1_down, g_mix, w_in, g_q_a, w_uq, g_kv_a, w_ukv,
           g_mla_qn, g_mla_qr, g_mla_kn, g_mla_kr, g_swa_q, g_swa_k, sink, w_o,
           g_ffn2, w2_gate, w2_up, w2_down):
    batch, seq, d = x.shape
    x2d = x.reshape(batch * seq, d)
    bf = lambda w: w.astype(BF16)
    row = lambda g: g.astype(F32)[None, :]

    x1 = _ffn_call(x2d, row(g_ffn1), bf(w1_gate), bf(w1_up), bf(w1_down))
    consts = _proj_constants(seq, g_mix, w_in, g_q_a, w_uq, g_kv_a, w_ukv,
                             g_mla_qn, g_mla_qr, g_mla_kn, g_mla_kr, g_swa_q, g_swa_k)
    q, k, v, sq, sk, sv = _proj_call(x1, batch, seq, consts)
    y_a = _mla_call(q, k, v, batch, seq)
    y_b = _swa_call(sink.astype(F32), sq, sk, sv, batch, seq)
    out = _ffn_call(x1, row(g_ffn2), bf(w2_gate), bf(w2_up), bf(w2_down),
                    mix=(y_a, y_b, bf(w_o)))
    return out.reshape(batch, seq, d)
```

```python
import math

import jax
import jax.numpy as jnp
from jax import lax
from jax.experimental import pallas as pl
from jax.experimental.pallas import tpu as pltpu

F32 = jnp.float32
BF16 = jnp.bfloat16

RMS_EPS = 1e-6
MLA_HEADS = 8
MLA_NOPE = 64
MLA_ROPE = 32
MLA_V = 64
ROPE_THETA = 10000.0
SWA_HEADS = 8
SWA_KV_HEADS = 2
SWA_HEAD_DIM = 64
SWA_GROUP = SWA_HEADS // SWA_KV_HEADS
WINDOW = 128
LOG2E = math.log2(math.e)

LANES = 128
V7X_VMEM_BYTES = 64 * 1024 * 1024
VMEM_HEADROOM_BYTES = 8 * 1024 * 1024

FFN_ROWS = 1024
FFN_CHUNK = 256
PROJ_ROWS = 1024
MLA_Q_ROWS = 1024
MLA_KEY_CHUNK = 512
SWA_Q_ROWS = 128
SWA_KEYS = SWA_Q_ROWS + 2 * WINDOW
SWA_SUB = 4


def _vmem_limit(block_bytes):
    want = 2 * block_bytes + 16 * 1024 * 1024
    return int(min(want, V7X_VMEM_BYTES - VMEM_HEADROOM_BYTES))


def _nbytes(shape, dtype):
    return math.prod(shape) * jnp.dtype(dtype).itemsize


def _const_spec(shape):
    zeros = (0,) * len(shape)
    return pl.BlockSpec(shape, lambda *_: zeros, pipeline_mode=pl.Buffered(1))


def _rms(x, gain):
    ms = jnp.mean(x * x, axis=-1, keepdims=True)
    return x * lax.rsqrt(ms + RMS_EPS) * gain


def _dot(a, b):
    return jnp.dot(a, b, preferred_element_type=F32)


def _dot_nt(a, b):
    return lax.dot_general(a, b, (((1,), (1,)), ((), ())), preferred_element_type=F32)


def _swiglu_residual(x, gain_ref, wg_ref, wu_ref, wd_ref, act_ref):
    h = _rms(x, gain_ref[...]).astype(BF16)
    d_ff = wg_ref.shape[1]
    for c in range(d_ff // FFN_CHUNK):
        cols = slice(c * FFN_CHUNK, (c + 1) * FFN_CHUNK)
        gate = _dot(h, wg_ref[:, cols])
        up = _dot(h, wu_ref[:, cols])
        act_ref[:, cols] = (gate / (1.0 + jnp.exp(-gate)) * up).astype(BF16)
    return x + 0.5 * _dot(act_ref[...], wd_ref[...])


def _ffn_kernel(x_ref, gain_ref, wg_ref, wu_ref, wd_ref, o_ref, act_ref):
    o_ref[...] = _swiglu_residual(x_ref[...], gain_ref, wg_ref, wu_ref, wd_ref, act_ref)


def _mix_ffn_kernel(x_ref, ya_ref, yb_ref, wo_ref, gain_ref, wg_ref, wu_ref, wd_ref,
                    o_ref, act_ref):
    half = ya_ref.shape[1]
    x2 = (x_ref[...] + _dot(ya_ref[...], wo_ref[:half, :])
          + _dot(yb_ref[...], wo_ref[half:, :]))
    o_ref[...] = _swiglu_residual(x2, gain_ref, wg_ref, wu_ref, wd_ref, act_ref)


def _ffn_call(x, gain, wg, wu, wd, mix=None):
    t, d = x.shape
    d_ff = wg.shape[1]
    assert t % FFN_ROWS == 0 and d_ff % FFN_CHUNK == 0
    row = lambda i: (i, 0)
    x_spec = pl.BlockSpec((FFN_ROWS, d), row)
    w_specs = [_const_spec((1, d)), _const_spec((d, d_ff)), _const_spec((d, d_ff)),
               _const_spec((d_ff, d))]
    block_bytes = (2 * _nbytes((FFN_ROWS, d), F32) + 3 * _nbytes((d, d_ff), BF16)
                   + _nbytes((FFN_ROWS, d_ff), BF16))
    if mix is None:
        body, operands, in_specs = _ffn_kernel, (x, gain, wg, wu, wd), [x_spec] + w_specs
    else:
        ya, yb, wo = mix
        half = ya.shape[1]
        y_spec = pl.BlockSpec((FFN_ROWS, half), row)
        body = _mix_ffn_kernel
        operands = (x, ya, yb, wo, gain, wg, wu, wd)
        in_specs = [x_spec, y_spec, y_spec, _const_spec(wo.shape)] + w_specs
        block_bytes += 2 * _nbytes((FFN_ROWS, half), BF16) + _nbytes(wo.shape, BF16)
    return pl.pallas_call(
        body,
        out_shape=jax.ShapeDtypeStruct((t, d), F32),
        grid=(t // FFN_ROWS,),
        in_specs=in_specs,
        out_specs=x_spec,
        scratch_shapes=[pltpu.VMEM((FFN_ROWS, d_ff), BF16)],
        compiler_params=pltpu.CompilerParams(
            dimension_semantics=("parallel",), vmem_limit_bytes=_vmem_limit(block_bytes)),
        name="ffn" if mix is None else "mix_ffn",
    )(*operands)


def _segment_mean_sq(x, seg_ref):
    return _dot((x * x).astype(BF16), seg_ref[...])


def _proj_kernel(x_ref, gmix_ref, win_ref, gqa_ref, wuq_ref, gkva_ref, wk_ref, wv_ref,
                 vones_ref, gq_ref, gk_ref, gkr_ref, gsq_ref, gsk_ref,
                 seg_mla_ref, seg_swa_ref, rc_ref, ra_ref, rb_ref,
                 q_ref, k_ref, v_ref, sq_ref, sk_ref, sv_ref):
    h = _rms(x_ref[...], gmix_ref[...]).astype(BF16)
    rc, ra, rb = rc_ref[...], ra_ref[...], rb_ref[...]

    def rope(t):
        return t * rc + pltpu.roll(t, LANES - 16, 1) * ra + pltpu.roll(t, 16, 1) * rb

    c_q = _rms(_dot(h, win_ref[:, 0:256]), gqa_ref[...]).astype(BF16)
    c_kv = _rms(_dot(h, win_ref[:, 256:384]), gkva_ref[...]).astype(BF16)
    hkr = _dot(h, win_ref[:, 384:512])
    kr_ms = jnp.sum(hkr * hkr, axis=-1, keepdims=True) * (1.0 / MLA_ROPE)
    k_pe = rope(hkr * lax.rsqrt(kr_ms + RMS_EPS) * gkr_ref[...])

    pairs = range(MLA_HEADS // 2)
    grp = lambda i: slice(i * 2 * LANES, (i + 1) * 2 * LANES)
    q_raw = [_dot(c_q, wuq_ref[:, grp(i)]) for i in pairs]
    k_raw = [_dot(c_kv, wk_ref[:, grp(i)]) for i in pairs]
    s_raw = [_dot(h, win_ref[:, 512 + i * 2 * LANES:512 + (i + 1) * 2 * LANES]) for i in range(3)]
    sv_ref[...] = _dot(h, win_ref[:, 1280:1536]).astype(BF16)
    for i in pairs:
        v_ref[:, grp(i)] = (_dot(c_kv, wv_ref[:, grp(i)]) + vones_ref[:, grp(i)]).astype(BF16)

    q_ms = [_segment_mean_sq(t, seg_mla_ref) for t in q_raw]
    k_ms = [_segment_mean_sq(t, seg_mla_ref) for t in k_raw]
    s_ms = [_segment_mean_sq(t, seg_swa_ref) for t in s_raw]

    for i in pairs:
        q = q_raw[i] * lax.rsqrt(q_ms[i] + RMS_EPS) * gq_ref[:, grp(i)]
        k = k_raw[i] * lax.rsqrt(k_ms[i] + RMS_EPS) * gk_ref[:, grp(i)]
        for j in range(2):
            blk = slice(j * LANES, (j + 1) * LANES)
            dst = slice((2 * i + j) * LANES, (2 * i + j + 1) * LANES)
            q_ref[:, dst] = rope(q[:, blk]).astype(BF16)
            k_ref[:, dst] = (k[:, blk] + k_pe).astype(BF16)
    for i in range(2):
        sq_ref[:, grp(i)] = (s_raw[i] * lax.rsqrt(s_ms[i] + RMS_EPS) * gsq_ref[:, grp(i)]).astype(BF16)
    sk_ref[...] = (s_raw[2] * lax.rsqrt(s_ms[2] + RMS_EPS) * gsk_ref[...]).astype(BF16)


def _proj_call(x1, batch, seq, consts):
    t, d = x1.shape
    assert seq % PROJ_ROWS == 0
    steps = seq // PROJ_ROWS
    row = lambda j, b: (b * steps + j, 0)
    pos = lambda j, b: (j, 0)
    in_specs = [pl.BlockSpec((PROJ_ROWS, d), row)]
    in_specs += [_const_spec(c.shape) for c in consts[:-3]]
    in_specs += [pl.BlockSpec((PROJ_ROWS, LANES), pos)] * 3
    widths = (MLA_HEADS * LANES, MLA_HEADS * LANES, MLA_HEADS * LANES,
              SWA_HEADS * SWA_HEAD_DIM, 2 * LANES, 2 * LANES)
    out_shape = [jax.ShapeDtypeStruct((t, w), BF16) for w in widths]
    out_specs = [pl.BlockSpec((PROJ_ROWS, w), row) for w in widths]
    block_bytes = (_nbytes((PROJ_ROWS, d), F32) + sum(_nbytes(c.shape, c.dtype) for c in consts[:-3])
                   + 3 * _nbytes((PROJ_ROWS, LANES), F32)
                   + sum(_nbytes((PROJ_ROWS, w), BF16) for w in widths))
    return pl.pallas_call(
        _proj_kernel,
        out_shape=out_shape,
        grid=(steps, batch),
        in_specs=in_specs,
        out_specs=out_specs,
        compiler_params=pltpu.CompilerParams(
            dimension_semantics=("parallel", "parallel"),
            vmem_limit_bytes=_vmem_limit(block_bytes)),
        name="proj",
    )(x1, *consts)


def _mla_kernel(q_ref, k_ref, v_ref, o_ref):
    lane = lax.broadcasted_iota(jnp.int32, (q_ref.shape[0], LANES), 1)

    def scores(h):
        blk = slice(h * LANES, (h + 1) * LANES)
        return _dot_nt(q_ref[:, blk], k_ref[:, blk])

    ahead = 1
    queue = [scores(h) for h in range(ahead)]
    outs = []
    for h in range(MLA_HEADS):
        s = queue.pop(0)
        if h + ahead < MLA_HEADS:
            queue.append(scores(h + ahead))
        m = jnp.max(s, axis=-1, keepdims=True)
        o = None
        for c in range(0, s.shape[1], MLA_KEY_CHUNK):
            p = jnp.exp2(s[:, c:c + MLA_KEY_CHUNK] - m).astype(BF16)
            part = _dot(p, v_ref[c:c + MLA_KEY_CHUNK, h * LANES:(h + 1) * LANES])
            o = part if o is None else o + part
        outs.append(o / pltpu.roll(o, LANES // 2, 1))
        if h % 2 == 1:
            o_ref[:, (h // 2) * LANES:(h // 2 + 1) * LANES] = (
                jnp.where(lane < LANES // 2, outs[h - 1], outs[h]).astype(BF16))


def _mla_call(q, k, v, batch, seq):
    t, width = q.shape
    assert seq % MLA_Q_ROWS == 0
    steps = seq // MLA_Q_ROWS
    q_spec = pl.BlockSpec((MLA_Q_ROWS, width), lambda b, i: (b * steps + i, 0))
    kv_spec = pl.BlockSpec((seq, width), lambda b, i: (b, 0))
    o_spec = pl.BlockSpec((MLA_Q_ROWS, MLA_HEADS * MLA_V), lambda b, i: (b * steps + i, 0))
    block_bytes = (2 * _nbytes((seq, width), BF16) + 2 * _nbytes((MLA_Q_ROWS, width), BF16)
                   + 4 * _nbytes((MLA_Q_ROWS, seq), F32))
    return pl.pallas_call(
        _mla_kernel,
        out_shape=jax.ShapeDtypeStruct((t, MLA_HEADS * MLA_V), BF16),
        grid=(batch, steps),
        in_specs=[q_spec, kv_spec, kv_spec],
        out_specs=o_spec,
        compiler_params=pltpu.CompilerParams(
            dimension_semantics=("parallel", "parallel"),
            vmem_limit_bytes=_vmem_limit(block_bytes)),
        name="mla_attn",
    )(q, k, v)


def _swa_key_start(i, seq):
    return jnp.clip(i * SWA_Q_ROWS - WINDOW, 0, seq - SWA_KEYS)


def _swa_bias_tables(seq):
    steps = seq // SWA_Q_ROWS
    r = jnp.arange(SWA_Q_ROWS)[:, None]
    c = jnp.arange(SWA_KEYS)[None, :]
    slopes = 2.0 ** (-(8.0 / SWA_HEADS) * jnp.arange(1, SWA_HEADS + 1, dtype=F32))
    cases = []
    for i in (0, 1, steps - 1):
        dist = jnp.abs(c - r + (_swa_key_start(i, seq) - i * SWA_Q_ROWS)).astype(F32)
        bias = -LOG2E * slopes[:, None, None] * dist[None]
        cases.append(jnp.where((dist <= WINDOW)[None], bias, -jnp.inf))
    return jnp.stack(cases)


def _swa_kernel(sink_ref, bias_ref, q_ref, k_ref, v_ref, o_ref):
    seq = k_ref.shape[0]
    blocks = seq // SWA_Q_ROWS
    rows = SWA_Q_ROWS
    lane = lax.broadcasted_iota(jnp.int32, (rows, LANES), 1)
    low_half = lane < LANES // 2
    kv_lane = lax.broadcasted_iota(jnp.int32, (SWA_KEYS, 2 * LANES), 1)
    low_kv = (kv_lane & (LANES // 2)) == 0

    pairs = range(SWA_GROUP // 2)
    q_block = lambda g, pr: slice((g * (SWA_GROUP // 2) + pr) * LANES,
                                  (g * (SWA_GROUP // 2) + pr + 1) * LANES)

    def score_units(u):
        blk = pl.program_id(1) * SWA_SUB + u
        key0 = pl.multiple_of(_swa_key_start(blk, seq), LANES)
        case = jnp.where(blk == 0, 0, jnp.where(blk == blocks - 1, 2, 1))
        qrows = slice(u * rows, (u + 1) * rows)
        kwin = k_ref[pl.ds(key0, SWA_KEYS), :].astype(F32)
        vwin = v_ref[pl.ds(key0, SWA_KEYS), :].astype(F32)
        k_lo = jnp.where(low_kv, kwin, 0.0).astype(BF16)
        k_hi = jnp.where(low_kv, 0.0, kwin).astype(BF16)
        v_lo = jnp.where(low_kv, vwin, 0.0).astype(BF16)
        v_hi = jnp.where(low_kv, 0.0, vwin).astype(BF16)
        units = []
        for g in range(SWA_KV_HEADS):
            grp = slice(g * LANES, (g + 1) * LANES)
            k_bd = jnp.concatenate([k_lo[:, grp], k_hi[:, grp]], axis=0)
            v_bd = jnp.concatenate([v_lo[:, grp], v_hi[:, grp]], axis=0)
            q2 = jnp.concatenate([q_ref[qrows, q_block(g, pr)] for pr in pairs], axis=0)
            units.append((g, qrows, case, _dot_nt(q2, k_bd), v_bd))
        return units

    def finish(unit):
        g, qrows, case, scores, v_bd = unit
        p_rows, denoms = [], []
        for pr in pairs:
            p_halves, d_halves = [], []
            for half in range(2):
                head = g * SWA_GROUP + 2 * pr + half
                logits = (scores[pr * rows:(pr + 1) * rows, half * SWA_KEYS:(half + 1) * SWA_KEYS]
                          + bias_ref[case, head])
                sink = sink_ref[head] * LOG2E
                m = jnp.maximum(jnp.max(logits, axis=-1, keepdims=True), sink)
                p = jnp.exp2(logits - m)
                d_halves.append(jnp.sum(p, axis=-1, keepdims=True) + jnp.exp2(sink - m))
                p_halves.append(p.astype(BF16))
            p_rows.append(jnp.concatenate(p_halves, axis=1))
            denoms.append(jnp.where(low_half, d_halves[0], d_halves[1]))
        o = _dot(jnp.concatenate(p_rows, axis=0), v_bd)
        for pr in pairs:
            o_ref[qrows, q_block(g, pr)] = (o[pr * rows:(pr + 1) * rows] / denoms[pr]).astype(BF16)

    pending = score_units(0)
    for u in range(SWA_SUB):
        current = pending
        if u + 1 < SWA_SUB:
            pending = score_units(u + 1)
        for unit in current:
            finish(unit)


def _swa_call(sink, sq, sk, sv, batch, seq):
    t, width = sq.shape
    step_rows = SWA_SUB * SWA_Q_ROWS
    assert seq % step_rows == 0 and seq >= SWA_KEYS and seq // SWA_Q_ROWS >= 3
    steps = seq // step_rows
    q_spec = pl.BlockSpec((step_rows, width), lambda b, i: (b * steps + i, 0))
    kv_spec = pl.BlockSpec((seq, 2 * LANES), lambda b, i: (b, 0))
    bias = _swa_bias_tables(seq)
    block_bytes = (2 * _nbytes((seq, 2 * LANES), BF16) + 2 * _nbytes((step_rows, width), BF16)
                   + _nbytes(bias.shape, F32)
                   + 2 * SWA_SUB * SWA_GROUP * _nbytes((SWA_Q_ROWS, SWA_KEYS), F32))
    return pl.pallas_call(
        _swa_kernel,
        out_shape=jax.ShapeDtypeStruct((t, width), BF16),
        grid=(batch, steps),
        in_specs=[pl.BlockSpec(memory_space=pltpu.SMEM), _const_spec(bias.shape),
                  q_spec, kv_spec, kv_spec],
        out_specs=q_spec,
        compiler_params=pltpu.CompilerParams(
            dimension_semantics=("parallel", "parallel"),
            vmem_limit_bytes=_vmem_limit(block_bytes)),
        name="swa_attn",
    )(sink, bias, sq, sk, sv)


def _head_blocks(w, heads, width, lo, hi, at=0):
    rows = w.shape[0]
    part = w.reshape(rows, heads, width)[:, :, lo:hi]
    part = jnp.pad(part, ((0, 0), (0, 0), (at, LANES - at - (hi - lo))))
    return part.reshape(rows, heads * LANES)


def _proj_constants(seq, g_mix, w_in, g_q_a, w_uq, g_kv_a, w_ukv,
                    g_qn, g_qr, g_kn, g_kr, g_swa_q, g_swa_k):
    d = w_in.shape[0]
    q_rank, kv_rank = g_q_a.shape[0], g_kv_a.shape[0]
    o_kr = q_rank + kv_rank
    o_sq = o_kr + MLA_ROPE
    o_sk = o_sq + SWA_HEADS * SWA_HEAD_DIM
    o_sv = o_sk + SWA_KV_HEADS * SWA_HEAD_DIM
    kr_block = jnp.pad(w_in[:, o_kr:o_sq], ((0, 0), (MLA_NOPE, LANES - MLA_NOPE - MLA_ROPE)))
    dup = lambda w: jnp.repeat(w.reshape(d, SWA_KV_HEADS, 1, SWA_HEAD_DIM), 2, axis=2).reshape(d, -1)
    win = jnp.concatenate(
        [w_in[:, :o_kr], kr_block, w_in[:, o_sq:o_sk], dup(w_in[:, o_sk:o_sv]), dup(w_in[:, o_sv:])],
        axis=1).astype(BF16)

    qk_width = MLA_NOPE + MLA_ROPE
    wuq = _head_blocks(w_uq, MLA_HEADS, qk_width, 0, qk_width).astype(BF16)
    kv_width = MLA_NOPE + MLA_V
    wk = _head_blocks(w_ukv, MLA_HEADS, kv_width, 0, MLA_NOPE).astype(BF16)
    wv_even = _head_blocks(w_ukv, MLA_HEADS, kv_width, MLA_NOPE, kv_width, at=0)
    wv_odd = _head_blocks(w_ukv, MLA_HEADS, kv_width, MLA_NOPE, kv_width, at=LANES // 2)
    head_of_col = jnp.arange(MLA_HEADS * LANES) // LANES
    odd_col = (head_of_col % 2 == 1)[None, :]
    wv = jnp.where(odd_col, wv_odd, wv_even).astype(BF16)
    lane = jnp.arange(MLA_HEADS * LANES) % LANES
    vones = ((lane >= LANES // 2) != odd_col[0]).astype(F32)[None, :]

    scale = LOG2E / math.sqrt(qk_width)
    zeros = lambda n: jnp.zeros((n,), F32)
    gq = jnp.tile(jnp.concatenate([g_qn, g_qr, zeros(LANES - qk_width)]) * scale, MLA_HEADS)[None, :]
    gk = jnp.tile(jnp.concatenate([g_kn, zeros(LANES - MLA_NOPE)]), MLA_HEADS)[None, :]
    gkr = jnp.concatenate([zeros(MLA_NOPE), g_kr, zeros(LANES - qk_width)])[None, :]
    gsq = jnp.tile(g_swa_q * (LOG2E / math.sqrt(SWA_HEAD_DIM)), SWA_HEADS)[None, :]
    gsk = jnp.tile(g_swa_k, 2 * SWA_KV_HEADS)[None, :]

    idx = jnp.arange(2 * LANES)
    in_blk = idx % LANES
    seg_id = jnp.where(in_blk < MLA_NOPE, 0, jnp.where(in_blk < qk_width, 1, 2)) + 3 * (idx // LANES)
    seg_len = jnp.where(in_blk < MLA_NOPE, MLA_NOPE, MLA_ROPE).astype(F32)
    same = (seg_id[:, None] == seg_id[None, :]) & (in_blk < qk_width)[:, None]
    seg_mla = jnp.where(same, 1.0 / seg_len[None, :], 0.0).astype(BF16)
    seg_swa = jnp.where((idx // SWA_HEAD_DIM)[:, None] == (idx // SWA_HEAD_DIM)[None, :],
                        1.0 / SWA_HEAD_DIM, 0.0).astype(BF16)

    pos = jnp.arange(seq, dtype=F32)
    inv = 1.0 / (ROPE_THETA ** (jnp.arange(0, MLA_ROPE, 2, dtype=F32) / MLA_ROPE))
    ang = pos[:, None] * inv[None, :]
    cos, sin = jnp.cos(ang), jnp.sin(ang)
    half = MLA_ROPE // 2
    pad_lo = jnp.ones((seq, MLA_NOPE), F32)
    pad_hi = jnp.ones((seq, LANES - qk_width), F32)
    rc = jnp.concatenate([pad_lo, cos, cos, pad_hi], axis=1)
    ra = jnp.concatenate([0 * pad_lo, -sin, jnp.zeros((seq, half), F32), 0 * pad_hi], axis=1)
    rb = jnp.concatenate([0 * pad_lo, jnp.zeros((seq, half), F32), sin, 0 * pad_hi], axis=1)

    row = lambda g: g.astype(F32)[None, :]
    return (row(g_mix), win, row(g_q_a), wuq, row(g_kv_a), wk, wv, vones,
            gq, gk, gkr, gsq, gsk, seg_mla, seg_swa, rc, ra, rb)


def kernel(x, g_ffn1, w1_gate, w1_up, w1_down, g_mix, w_in, g_q_a, w_uq, g_kv_a, w_ukv,
           g_mla_qn, g_mla_qr, g_mla_kn, g_mla_kr, g_swa_q, g_swa_k, sink, w_o,
           g_ffn2, w2_gate, w2_up, w2_down):
    batch, seq, d = x.shape
    x2d = x.reshape(batch * seq, d)
    bf = lambda w: w.astype(BF16)
    row = lambda g: g.astype(F32)[None, :]

    x1 = _ffn_call(x2d, row(g_ffn1), bf(w1_gate), bf(w1_up), bf(w1_down))
    consts = _proj_constants(seq, g_mix, w_in, g_q_a, w_uq, g_kv_a, w_ukv,
                             g_mla_qn, g_mla_qr, g_mla_kn, g_mla_kr, g_swa_q, g_swa_k)
    q, k, v, sq, sk, sv = _proj_call(x1, batch, seq, consts)
    y_a = _mla_call(q, k, v, batch, seq)
    y_b = _swa_call(sink.astype(F32), sq, sk, sv, batch, seq)
    out = _ffn_call(x1, row(g_ffn2), bf(w2_gate), bf(w2_up), bf(w2_down),
                    mix=(y_a, y_b, bf(w_o)))
    return out.reshape(batch, seq, d)
```

```python
import math

import jax
import jax.numpy as jnp
from jax import lax
from jax.experimental import pallas as pl
from jax.experimental.pallas import tpu as pltpu

F32 = jnp.float32
BF16 = jnp.bfloat16

RMS_EPS = 1e-6
MLA_HEADS = 8
MLA_NOPE = 64
MLA_ROPE = 32
MLA_V = 64
ROPE_THETA = 10000.0
SWA_HEADS = 8
SWA_KV_HEADS = 2
SWA_HEAD_DIM = 64
SWA_GROUP = SWA_HEADS // SWA_KV_HEADS
WINDOW = 128
LOG2E = math.log2(math.e)

LANES = 128
V7X_VMEM_BYTES = 64 * 1024 * 1024
VMEM_HEADROOM_BYTES = 8 * 1024 * 1024

FFN_ROWS = 1024
FFN_ROW_PARTS = 2
FFN_CHUNK = 256
PROJ_ROWS = 1024
PROJ_CHUNKS = 4
MLA_ROWS = 1024
MLA_ROW_PARTS = 1
MLA_KEY_CHUNK = 512
SWA_ROWS = 1024
SWA_Q_ROWS = 128
SWA_KEYS = SWA_Q_ROWS + 2 * WINDOW


def _vmem_limit(block_bytes):
    want = 2 * block_bytes + 16 * 1024 * 1024
    return int(min(want, V7X_VMEM_BYTES - VMEM_HEADROOM_BYTES))


def _nbytes(shape, dtype):
    return math.prod(shape) * jnp.dtype(dtype).itemsize


def _const_spec(shape):
    zeros = (0,) * len(shape)
    return pl.BlockSpec(shape, lambda *_: zeros, pipeline_mode=pl.Buffered(1))


def _rms(x, gain):
    ms = jnp.mean(x * x, axis=-1, keepdims=True)
    return x * lax.rsqrt(ms + RMS_EPS) * gain


def _dot(a, b):
    return jnp.dot(a, b, preferred_element_type=F32)


def _dot_nt(a, b):
    return lax.dot_general(a, b, (((1,), (1,)), ((), ())), preferred_element_type=F32)


def _row_parts(rows):
    part = rows // FFN_ROW_PARTS
    return [slice(i * part, (i + 1) * part) for i in range(FFN_ROW_PARTS)]


def _swiglu_residual(xs, parts, gain_ref, wg_ref, wu_ref, wd_ref, act_ref, o_ref):
    hs = [_rms(x, gain_ref[...]).astype(BF16) for x in xs]
    d_ff = wg_ref.shape[1]
    for x, h, rows in zip(xs, hs, parts):
        for c in range(d_ff // FFN_CHUNK):
            cols = slice(c * FFN_CHUNK, (c + 1) * FFN_CHUNK)
            gate = _dot(h, wg_ref[:, cols])
            up = _dot(h, wu_ref[:, cols])
            act_ref[rows, cols] = (gate / (1.0 + jnp.exp(-gate)) * up).astype(BF16)
        o_ref[rows, :] = x + 0.5 * _dot(act_ref[rows, :], wd_ref[...])


def _ffn_kernel(x_ref, gain_ref, wg_ref, wu_ref, wd_ref, o_ref, act_ref):
    parts = _row_parts(x_ref.shape[0])
    xs = [x_ref[rows, :] for rows in parts]
    _swiglu_residual(xs, parts, gain_ref, wg_ref, wu_ref, wd_ref, act_ref, o_ref)


def _mix_ffn_kernel(x_ref, ya_ref, yb_ref, wo_ref, gain_ref, wg_ref, wu_ref, wd_ref,
                    o_ref, act_ref):
    half = ya_ref.shape[1]
    parts = _row_parts(x_ref.shape[0])
    xs = [x_ref[rows, :] + _dot(ya_ref[rows, :], wo_ref[:half, :])
          + _dot(yb_ref[rows, :], wo_ref[half:, :]) for rows in parts]
    _swiglu_residual(xs, parts, gain_ref, wg_ref, wu_ref, wd_ref, act_ref, o_ref)


def _ffn_call(x, gain, wg, wu, wd, mix=None):
    t, d = x.shape
    d_ff = wg.shape[1]
    assert t % FFN_ROWS == 0 and d_ff % FFN_CHUNK == 0
    row = lambda i: (i, 0)
    x_spec = pl.BlockSpec((FFN_ROWS, d), row)
    w_specs = [_const_spec((1, d)), _const_spec((d, d_ff)), _const_spec((d, d_ff)),
               _const_spec((d_ff, d))]
    block_bytes = (2 * _nbytes((FFN_ROWS, d), F32) + 3 * _nbytes((d, d_ff), BF16)
                   + _nbytes((FFN_ROWS, d_ff), BF16))
    if mix is None:
        body, operands, in_specs = _ffn_kernel, (x, gain, wg, wu, wd), [x_spec] + w_specs
    else:
        ya, yb, wo = mix
        half = ya.shape[1]
        y_spec = pl.BlockSpec((FFN_ROWS, half), row)
        body = _mix_ffn_kernel
        operands = (x, ya, yb, wo, gain, wg, wu, wd)
        in_specs = [x_spec, y_spec, y_spec, _const_spec(wo.shape)] + w_specs
        block_bytes += 2 * _nbytes((FFN_ROWS, half), BF16) + _nbytes(wo.shape, BF16)
    return pl.pallas_call(
        body,
        out_shape=jax.ShapeDtypeStruct((t, d), F32),
        grid=(t // FFN_ROWS,),
        in_specs=in_specs,
        out_specs=x_spec,
        scratch_shapes=[pltpu.VMEM((FFN_ROWS, d_ff), BF16)],
        compiler_params=pltpu.CompilerParams(
            dimension_semantics=("parallel",), vmem_limit_bytes=_vmem_limit(block_bytes)),
        name="ffn" if mix is None else "mix_ffn",
    )(*operands)


def _segment_mean_sq(x, seg_ref):
    return _dot((x * x).astype(BF16), seg_ref[...])


def _proj_kernel(x_ref, gmix_ref, win_ref, gqa_ref, wuq_ref, gkva_ref, wk_ref, wv_ref,
                 vones_ref, gq_ref, gk_ref, gkr_ref, gsq_ref, gsk_ref,
                 seg_mla_ref, seg_swa_ref, rc_ref, ra_ref, rb_ref,
                 q_ref, k_ref, v_ref, sq_ref, sk_ref, sv_ref):
    pairs = range(MLA_HEADS // 2)
    grp = lambda i: slice(i * 2 * LANES, (i + 1) * 2 * LANES)

    def reduce_stage(rows):
        h = _rms(x_ref[rows, :], gmix_ref[...]).astype(BF16)
        rc, ra, rb = rc_ref[rows, :], ra_ref[rows, :], rb_ref[rows, :]

        def rope(t):
            return t * rc + pltpu.roll(t, LANES - 16, 1) * ra + pltpu.roll(t, 16, 1) * rb

        c_q = _rms(_dot(h, win_ref[:, 0:256]), gqa_ref[...]).astype(BF16)
        c_kv = _rms(_dot(h, win_ref[:, 256:384]), gkva_ref[...]).astype(BF16)
        hkr = _dot(h, win_ref[:, 384:512])
        kr_ms = jnp.sum(hkr * hkr, axis=-1, keepdims=True) * (1.0 / MLA_ROPE)
        k_pe = rope(hkr * lax.rsqrt(kr_ms + RMS_EPS) * gkr_ref[...])
        return dict(rows=rows, h=h, c_q=c_q, c_kv=c_kv, k_pe=k_pe, rope=rope)

    def raw_stage(st):
        h, c_q, c_kv, rows = st["h"], st["c_q"], st["c_kv"], st["rows"]
        st["q_raw"] = [_dot(c_q, wuq_ref[:, grp(i)]) for i in pairs]
        st["k_raw"] = [_dot(c_kv, wk_ref[:, grp(i)]) for i in pairs]
        st["s_raw"] = [_dot(h, win_ref[:, 512 + i * 2 * LANES:512 + (i + 1) * 2 * LANES])
                       for i in range(3)]
        sv_ref[rows, :] = _dot(h, win_ref[:, 1280:1536]).astype(BF16)
        for i in pairs:
            v_ref[rows, grp(i)] = (_dot(c_kv, wv_ref[:, grp(i)]) + vones_ref[:, grp(i)]).astype(BF16)

    def stats_stage(st):
        st["q_ms"] = [_segment_mean_sq(t, seg_mla_ref) for t in st["q_raw"]]
        st["k_ms"] = [_segment_mean_sq(t, seg_mla_ref) for t in st["k_raw"]]
        st["s_ms"] = [_segment_mean_sq(t, seg_swa_ref) for t in st["s_raw"]]

    def store_stage(st):
        rows, rope = st["rows"], st["rope"]
        for i in pairs:
            q = st["q_raw"][i] * lax.rsqrt(st["q_ms"][i] + RMS_EPS) * gq_ref[:, grp(i)]
            k = st["k_raw"][i] * lax.rsqrt(st["k_ms"][i] + RMS_EPS) * gk_ref[:, grp(i)]
            for j in range(2):
                blk = slice(j * LANES, (j + 1) * LANES)
                dst = slice((2 * i + j) * LANES, (2 * i + j + 1) * LANES)
                q_ref[rows, dst] = rope(q[:, blk]).astype(BF16)
                k_ref[rows, dst] = (k[:, blk] + st["k_pe"]).astype(BF16)
        s_raw, s_ms = st["s_raw"], st["s_ms"]
        for i in range(2):
            sq_ref[rows, grp(i)] = (s_raw[i] * lax.rsqrt(s_ms[i] + RMS_EPS) * gsq_ref[:, grp(i)]).astype(BF16)
        sk_ref[rows, :] = (s_raw[2] * lax.rsqrt(s_ms[2] + RMS_EPS) * gsk_ref[...]).astype(BF16)

    chunk = PROJ_ROWS // PROJ_CHUNKS
    stages = (raw_stage, stats_stage, store_stage)
    states = []
    for step in range(PROJ_CHUNKS + len(stages)):
        if step < PROJ_CHUNKS:
            states.append(reduce_stage(slice(step * chunk, (step + 1) * chunk)))
        for age, stage in enumerate(stages):
            c = step - 1 - age
            if 0 <= c < PROJ_CHUNKS:
                stage(states[c])


def _proj_call(x1, batch, seq, consts):
    t, d = x1.shape
    assert seq % PROJ_ROWS == 0
    steps = seq // PROJ_ROWS
    row = lambda j, b: (b * steps + j, 0)
    pos = lambda j, b: (j, 0)
    in_specs = [pl.BlockSpec((PROJ_ROWS, d), row)]
    in_specs += [_const_spec(c.shape) for c in consts[:-3]]
    in_specs += [pl.BlockSpec((PROJ_ROWS, LANES), pos)] * 3
    widths = (MLA_HEADS * LANES, MLA_HEADS * LANES, MLA_HEADS * LANES,
              SWA_HEADS * SWA_HEAD_DIM, 2 * LANES, 2 * LANES)
    out_shape = [jax.ShapeDtypeStruct((t, w), BF16) for w in widths]
    out_specs = [pl.BlockSpec((PROJ_ROWS, w), row) for w in widths]
    block_bytes = (_nbytes((PROJ_ROWS, d), F32) + sum(_nbytes(c.shape, c.dtype) for c in consts[:-3])
                   + 3 * _nbytes((PROJ_ROWS, LANES), F32)
                   + sum(_nbytes((PROJ_ROWS, w), BF16) for w in widths))
    return pl.pallas_call(
        _proj_kernel,
        out_shape=out_shape,
        grid=(steps, batch),
        in_specs=in_specs,
        out_specs=out_specs,
        compiler_params=pltpu.CompilerParams(
            dimension_semantics=("parallel", "parallel"),
            vmem_limit_bytes=_vmem_limit(block_bytes)),
        name="proj",
    )(x1, *consts)


def _mla_stages(q_ref, k_ref, v_ref, o_ref):
    part_rows = q_ref.shape[0] // MLA_ROW_PARTS
    lane = lax.broadcasted_iota(jnp.int32, (part_rows, LANES), 1)
    outs = {}

    def scores(unit):
        part, h = unit
        blk = slice(h * LANES, (h + 1) * LANES)
        rows = slice(part * part_rows, (part + 1) * part_rows)
        return _dot_nt(q_ref[rows, blk], k_ref[:, blk])

    def finish(unit, s):
        part, h = unit
        rows = slice(part * part_rows, (part + 1) * part_rows)
        m = jnp.max(s, axis=-1, keepdims=True)
        o = None
        for c in range(0, s.shape[1], MLA_KEY_CHUNK):
            p = jnp.exp2(s[:, c:c + MLA_KEY_CHUNK] - m).astype(BF16)
            pv = _dot(p, v_ref[c:c + MLA_KEY_CHUNK, h * LANES:(h + 1) * LANES])
            o = pv if o is None else o + pv
        outs[unit] = o / pltpu.roll(o, LANES // 2, 1)
        if h % 2 == 1:
            o_ref[rows, (h // 2) * LANES:(h // 2 + 1) * LANES] = (
                jnp.where(lane < LANES // 2, outs[(part, h - 1)], outs[unit]).astype(BF16))

    return scores, finish


def _mla_kernel(q_ref, k_ref, v_ref, o_ref):
    scores, finish = _mla_stages(q_ref, k_ref, v_ref, o_ref)
    units = [(part, h) for part in range(MLA_ROW_PARTS) for h in range(MLA_HEADS)]
    s = scores(units[0])
    for i, unit in enumerate(units):
        s_next = scores(units[i + 1]) if i + 1 < len(units) else None
        finish(unit, s)
        s = s_next


def _mla_call(q, k, v, batch, seq):
    t, width = q.shape
    assert seq % MLA_ROWS == 0
    steps = seq // MLA_ROWS
    q_spec = pl.BlockSpec((MLA_ROWS, width), lambda b, i: (b * steps + i, 0))
    kv_spec = pl.BlockSpec((seq, width), lambda b, i: (b, 0))
    o_spec = pl.BlockSpec((MLA_ROWS, MLA_HEADS * MLA_V), lambda b, i: (b * steps + i, 0))
    block_bytes = (2 * _nbytes((seq, width), BF16) + 2 * _nbytes((MLA_ROWS, width), BF16)
                   + 4 * _nbytes((MLA_ROWS, seq), F32))
    return pl.pallas_call(
        _mla_kernel,
        out_shape=jax.ShapeDtypeStruct((t, MLA_HEADS * MLA_V), BF16),
        grid=(batch, steps),
        in_specs=[q_spec, kv_spec, kv_spec],
        out_specs=o_spec,
        compiler_params=pltpu.CompilerParams(
            dimension_semantics=("parallel", "parallel"),
            vmem_limit_bytes=_vmem_limit(block_bytes)),
        name="mla_attn",
    )(q, k, v)


def _swa_key_start(i, seq):
    return jnp.clip(i * SWA_Q_ROWS - WINDOW, 0, seq - SWA_KEYS)


def _swa_bias_tables(seq):
    steps = seq // SWA_Q_ROWS
    r = jnp.arange(SWA_Q_ROWS)[:, None]
    c = jnp.arange(SWA_KEYS)[None, :]
    slopes = 2.0 ** (-(8.0 / SWA_HEADS) * jnp.arange(1, SWA_HEADS + 1, dtype=F32))
    cases = []
    for i in (0, 1, steps - 1):
        dist = jnp.abs(c - r + (_swa_key_start(i, seq) - i * SWA_Q_ROWS)).astype(F32)
        bias = -LOG2E * slopes[:, None, None] * dist[None]
        cases.append(jnp.where((dist <= WINDOW)[None], bias, -jnp.inf))
    return jnp.stack(cases)


def _swa_stages(sink_ref, bias_ref, q_ref, k_ref, v_ref, o_ref):
    seq = k_ref.shape[0]
    blocks = seq // SWA_Q_ROWS
    rows = SWA_Q_ROWS
    blocks_per_step = q_ref.shape[0] // rows
    lane = lax.broadcasted_iota(jnp.int32, (rows, LANES), 1)
    low_half = lane < LANES // 2
    kv_lane = lax.broadcasted_iota(jnp.int32, (SWA_KEYS, 2 * LANES), 1)
    low_kv = (kv_lane & (LANES // 2)) == 0

    pairs = range(SWA_GROUP // 2)
    q_block = lambda g, pr: slice((g * (SWA_GROUP // 2) + pr) * LANES,
                                  (g * (SWA_GROUP // 2) + pr + 1) * LANES)

    def score_units(u):
        blk = pl.program_id(1) * blocks_per_step + u
        key0 = pl.multiple_of(_swa_key_start(blk, seq), LANES)
        case = jnp.where(blk == 0, 0, jnp.where(blk == blocks - 1, 2, 1))
        qrows = slice(u * rows, (u + 1) * rows)
        kwin = k_ref[pl.ds(key0, SWA_KEYS), :].astype(F32)
        vwin = v_ref[pl.ds(key0, SWA_KEYS), :].astype(F32)
        k_lo = jnp.where(low_kv, kwin, 0.0).astype(BF16)
        k_hi = jnp.where(low_kv, 0.0, kwin).astype(BF16)
        v_lo = jnp.where(low_kv, vwin, 0.0).astype(BF16)
        v_hi = jnp.where(low_kv, 0.0, vwin).astype(BF16)
        units = []
        for g in range(SWA_KV_HEADS):
            grp = slice(g * LANES, (g + 1) * LANES)
            k_bd = jnp.concatenate([k_lo[:, grp], k_hi[:, grp]], axis=0)
            v_bd = jnp.concatenate([v_lo[:, grp], v_hi[:, grp]], axis=0)
            q2 = jnp.concatenate([q_ref[qrows, q_block(g, pr)] for pr in pairs], axis=0)
            units.append((g, qrows, case, _dot_nt(q2, k_bd), v_bd))
        return units

    def finish(unit):
        g, qrows, case, scores, v_bd = unit
        p_rows, denoms = [], []
        for pr in pairs:
            p_halves, d_halves = [], []
            for half in range(2):
                head = g * SWA_GROUP + 2 * pr + half
                logits = (scores[pr * rows:(pr + 1) * rows, half * SWA_KEYS:(half + 1) * SWA_KEYS]
                          + bias_ref[case, head])
                sink = sink_ref[head] * LOG2E
                m = jnp.maximum(jnp.max(logits, axis=-1, keepdims=True), sink)
                p = jnp.exp2(logits - m)
                d_halves.append(jnp.sum(p, axis=-1, keepdims=True) + jnp.exp2(sink - m))
                p_halves.append(p.astype(BF16))
            p_rows.append(jnp.concatenate(p_halves, axis=1))
            denoms.append(jnp.where(low_half, d_halves[0], d_halves[1]))
        o = _dot(jnp.concatenate(p_rows, axis=0), v_bd)
        for pr in pairs:
            o_ref[qrows, q_block(g, pr)] = (o[pr * rows:(pr + 1) * rows] / denoms[pr]).astype(BF16)

    return score_units, finish


def _swa_kernel(sink_ref, bias_ref, q_ref, k_ref, v_ref, o_ref):
    score_units, finish = _swa_stages(sink_ref, bias_ref, q_ref, k_ref, v_ref, o_ref)
    blocks = q_ref.shape[0] // SWA_Q_ROWS
    pending = score_units(0)
    for u in range(blocks):
        current = pending
        if u + 1 < blocks:
            pending = score_units(u + 1)
        for unit in current:
            finish(unit)


def _swa_call(sink, sq, sk, sv, batch, seq):
    t, width = sq.shape
    assert seq % SWA_ROWS == 0 and SWA_ROWS % SWA_Q_ROWS == 0
    assert seq >= SWA_KEYS and seq // SWA_Q_ROWS >= 3
    steps = seq // SWA_ROWS
    q_spec = pl.BlockSpec((SWA_ROWS, width), lambda b, i: (b * steps + i, 0))
    kv_spec = pl.BlockSpec((seq, 2 * LANES), lambda b, i: (b, 0))
    bias = _swa_bias_tables(seq)
    block_bytes = (2 * _nbytes((seq, 2 * LANES), BF16) + 2 * _nbytes((SWA_ROWS, width), BF16)
                   + _nbytes(bias.shape, F32) + 4 * _nbytes((SWA_ROWS, 2 * SWA_KEYS), F32))
    return pl.pallas_call(
        _swa_kernel,
        out_shape=jax.ShapeDtypeStruct((t, width), BF16),
        grid=(batch, steps),
        in_specs=[pl.BlockSpec(memory_space=pltpu.SMEM), _const_spec(bias.shape),
                  q_spec, kv_spec, kv_spec],
        out_specs=q_spec,
        compiler_params=pltpu.CompilerParams(
            dimension_semantics=("parallel", "parallel"),
            vmem_limit_bytes=_vmem_limit(block_bytes)),
        name="swa_attn",
    )(sink, bias, sq, sk, sv)


def _head_blocks(w, heads, width, lo, hi, at=0):
    rows = w.shape[0]
    part = w.reshape(rows, heads, width)[:, :, lo:hi]
    part = jnp.pad(part, ((0, 0), (0, 0), (at, LANES - at - (hi - lo))))
    return part.reshape(rows, heads * LANES)


def _proj_constants(seq, g_mix, w_in, g_q_a, w_uq, g_kv_a, w_ukv,
                    g_qn, g_qr, g_kn, g_kr, g_swa_q, g_swa_k):
    d = w_in.shape[0]
    q_rank, kv_rank = g_q_a.shape[0], g_kv_a.shape[0]
    o_kr = q_rank + kv_rank
    o_sq = o_kr + MLA_ROPE
    o_sk = o_sq + SWA_HEADS * SWA_HEAD_DIM
    o_sv = o_sk + SWA_KV_HEADS * SWA_HEAD_DIM
    kr_block = jnp.pad(w_in[:, o_kr:o_sq], ((0, 0), (MLA_NOPE, LANES - MLA_NOPE - MLA_ROPE)))
    dup = lambda w: jnp.repeat(w.reshape(d, SWA_KV_HEADS, 1, SWA_HEAD_DIM), 2, axis=2).reshape(d, -1)
    win = jnp.concatenate(
        [w_in[:, :o_kr], kr_block, w_in[:, o_sq:o_sk], dup(w_in[:, o_sk:o_sv]), dup(w_in[:, o_sv:])],
        axis=1).astype(BF16)

    qk_width = MLA_NOPE + MLA_ROPE
    wuq = _head_blocks(w_uq, MLA_HEADS, qk_width, 0, qk_width).astype(BF16)
    kv_width = MLA_NOPE + MLA_V
    wk = _head_blocks(w_ukv, MLA_HEADS, kv_width, 0, MLA_NOPE).astype(BF16)
    wv_even = _head_blocks(w_ukv, MLA_HEADS, kv_width, MLA_NOPE, kv_width, at=0)
    wv_odd = _head_blocks(w_ukv, MLA_HEADS, kv_width, MLA_NOPE, kv_width, at=LANES // 2)
    head_of_col = jnp.arange(MLA_HEADS * LANES) // LANES
    odd_col = (head_of_col % 2 == 1)[None, :]
    wv = jnp.where(odd_col, wv_odd, wv_even).astype(BF16)
    lane = jnp.arange(MLA_HEADS * LANES) % LANES
    vones = ((lane >= LANES // 2) != odd_col[0]).astype(F32)[None, :]

    scale = LOG2E / math.sqrt(qk_width)
    zeros = lambda n: jnp.zeros((n,), F32)
    gq = jnp.tile(jnp.concatenate([g_qn, g_qr, zeros(LANES - qk_width)]) * scale, MLA_HEADS)[None, :]
    gk = jnp.tile(jnp.concatenate([g_kn, zeros(LANES - MLA_NOPE)]), MLA_HEADS)[None, :]
    gkr = jnp.concatenate([zeros(MLA_NOPE), g_kr, zeros(LANES - qk_width)])[None, :]
    gsq = jnp.tile(g_swa_q * (LOG2E / math.sqrt(SWA_HEAD_DIM)), SWA_HEADS)[None, :]
    gsk = jnp.tile(g_swa_k, 2 * SWA_KV_HEADS)[None, :]

    idx = jnp.arange(2 * LANES)
    in_blk = idx % LANES
    seg_id = jnp.where(in_blk < MLA_NOPE, 0, jnp.where(in_blk < qk_width, 1, 2)) + 3 * (idx // LANES)
    seg_len = jnp.where(in_blk < MLA_NOPE, MLA_NOPE, MLA_ROPE).astype(F32)
    same = (seg_id[:, None] == seg_id[None, :]) & (in_blk < qk_width)[:, None]
    seg_mla = jnp.where(same, 1.0 / seg_len[None, :], 0.0).astype(BF16)
    seg_swa = jnp.where((idx // SWA_HEAD_DIM)[:, None] == (idx // SWA_HEAD_DIM)[None, :],
                        1.0 / SWA_HEAD_DIM, 0.0).astype(BF16)

    pos = jnp.arange(seq, dtype=F32)
    inv = 1.0 / (ROPE_THETA ** (jnp.arange(0, MLA_ROPE, 2, dtype=F32) / MLA_ROPE))
    ang = pos[:, None] * inv[None, :]
    cos, sin = jnp.cos(ang), jnp.sin(ang)
    half = MLA_ROPE // 2
    pad_lo = jnp.ones((seq, MLA_NOPE), F32)
    pad_hi = jnp.ones((seq, LANES - qk_width), F32)
    rc = jnp.concatenate([pad_lo, cos, cos, pad_hi], axis=1)
    ra = jnp.concatenate([0 * pad_lo, -sin, jnp.zeros((seq, half), F32), 0 * pad_hi], axis=1)
    rb = jnp.concatenate([0 * pad_lo, jnp.zeros((seq, half), F32), sin, 0 * pad_hi], axis=1)

    row = lambda g: g.astype(F32)[None, :]
    return (row(g_mix), win, row(g_q_a), wuq, row(g_kv_a), wk, wv, vones,
            gq, gk, gkr, gsq, gsk, seg_mla, seg_swa, rc, ra, rb)


def kernel(x, g_ffn1, w1_gate, w1_up, w1_down, g_mix, w_in, g_q_a, w_uq, g_kv_a, w_ukv,
           g_mla_qn, g_mla_qr, g_mla_kn, g_mla_kr, g_swa_q, g_swa_k, sink, w_o,
           g_ffn2, w2_gate, w2_up, w2_down):
    batch, seq, d = x.shape
    x2d = x.reshape(batch * seq, d)
    bf = lambda w: w.astype(BF16)
    row = lambda g: g.astype(F32)[None, :]

    x1 = _ffn_call(x2d, row(g_ffn1), bf(w1_gate), bf(w1_up), bf(w1_down))
    consts = _proj_constants(seq, g_mix, w_in, g_q_a, w_uq, g_kv_a, w_ukv,
                             g_mla_qn, g_mla_qr, g_mla_kn, g_mla_kr, g_swa_q, g_swa_k)
    q, k, v, sq, sk, sv = _proj_call(x1, batch, seq, consts)
    y_a = _mla_call(q, k, v, batch, seq)
    y_b = _swa_call(sink.astype(F32), sq, sk, sv, batch, seq)
    out = _ffn_call(x1, row(g_ffn2), bf(w2_gate), bf(w2_up), bf(w2_down),
                    mix=(y_a, y_b, bf(w_o)))
    return out.reshape(batch, seq, d)
```

```python
import math

import jax
import jax.numpy as jnp
from jax import lax
from jax.experimental import pallas as pl
from jax.experimental.pallas import tpu as pltpu

F32 = jnp.float32
BF16 = jnp.bfloat16

RMS_EPS = 1e-6
MLA_HEADS = 8
MLA_NOPE = 64
MLA_ROPE = 32
MLA_V = 64
ROPE_THETA = 10000.0
SWA_HEADS = 8
SWA_KV_HEADS = 2
SWA_HEAD_DIM = 64
SWA_GROUP = SWA_HEADS // SWA_KV_HEADS
WINDOW = 128
LOG2E = math.log2(math.e)

LANES = 128
V7X_VMEM_BYTES = 64 * 1024 * 1024
VMEM_HEADROOM_BYTES = 8 * 1024 * 1024

FFN_ROWS = 512
FFN_ROW_PARTS = 1
FFN_CHUNK = 256
PROJ_ROWS = 1024
PROJ_CHUNKS = 1
MLA_ROWS = 1024
MLA_ROW_PARTS = 1
MLA_KEY_CHUNK = 512
SWA_ROWS = 1024
SWA_Q_ROWS = 128
SWA_KEYS = SWA_Q_ROWS + 2 * WINDOW


def _vmem_limit(block_bytes):
    want = 2 * block_bytes + 16 * 1024 * 1024
    return int(min(want, V7X_VMEM_BYTES - VMEM_HEADROOM_BYTES))


def _nbytes(shape, dtype):
    return math.prod(shape) * jnp.dtype(dtype).itemsize


def _const_spec(shape):
    zeros = (0,) * len(shape)
    return pl.BlockSpec(shape, lambda *_: zeros, pipeline_mode=pl.Buffered(1))


def _rms(x, gain):
    ms = jnp.mean(x * x, axis=-1, keepdims=True)
    return x * lax.rsqrt(ms + RMS_EPS) * gain


def _dot(a, b):
    return jnp.dot(a, b, preferred_element_type=F32)


def _dot_nt(a, b):
    return lax.dot_general(a, b, (((1,), (1,)), ((), ())), preferred_element_type=F32)


def _row_parts(rows):
    part = rows // FFN_ROW_PARTS
    return [slice(i * part, (i + 1) * part) for i in range(FFN_ROW_PARTS)]


def _swiglu_residual(xs, parts, gain_ref, wg_ref, wu_ref, wd_ref, act_ref, o_ref):
    hs = [_rms(x, gain_ref[...]) for x in xs]
    d_ff = wg_ref.shape[1]
    for x, h, rows in zip(xs, hs, parts):
        for c in range(d_ff // FFN_CHUNK):
            cols = slice(c * FFN_CHUNK, (c + 1) * FFN_CHUNK)
            gate = _dot(h, wg_ref[:, cols])
            up = _dot(h, wu_ref[:, cols])
            act_ref[rows, cols] = gate / (1.0 + jnp.exp(-gate)) * up
        o_ref[rows, :] = x + 0.5 * _dot(act_ref[rows, :], wd_ref[...])


def _ffn_kernel(x_ref, gain_ref, wg_ref, wu_ref, wd_ref, o_ref, act_ref):
    parts = _row_parts(x_ref.shape[0])
    xs = [x_ref[rows, :] for rows in parts]
    _swiglu_residual(xs, parts, gain_ref, wg_ref, wu_ref, wd_ref, act_ref, o_ref)


def _mix_ffn_kernel(x_ref, ya_ref, yb_ref, wo_ref, gain_ref, wg_ref, wu_ref, wd_ref,
                    o_ref, act_ref):
    half = ya_ref.shape[1]
    parts = _row_parts(x_ref.shape[0])
    xs = [x_ref[rows, :] + _dot(ya_ref[rows, :], wo_ref[:half, :])
          + _dot(yb_ref[rows, :], wo_ref[half:, :]) for rows in parts]
    _swiglu_residual(xs, parts, gain_ref, wg_ref, wu_ref, wd_ref, act_ref, o_ref)


def _ffn_call(x, gain, wg, wu, wd, mix=None):
    t, d = x.shape
    d_ff = wg.shape[1]
    assert t % FFN_ROWS == 0 and d_ff % FFN_CHUNK == 0
    row = lambda i: (i, 0)
    x_spec = pl.BlockSpec((FFN_ROWS, d), row)
    w_specs = [_const_spec((1, d)), _const_spec((d, d_ff)), _const_spec((d, d_ff)),
               _const_spec((d_ff, d))]
    block_bytes = (2 * _nbytes((FFN_ROWS, d), F32) + 3 * _nbytes((d, d_ff), BF16)
                   + _nbytes((FFN_ROWS, d_ff), BF16))
    if mix is None:
        body, operands, in_specs = _ffn_kernel, (x, gain, wg, wu, wd), [x_spec] + w_specs
    else:
        ya, yb, wo = mix
        half = ya.shape[1]
        y_spec = pl.BlockSpec((FFN_ROWS, half), row)
        body = _mix_ffn_kernel
        operands = (x, ya, yb, wo, gain, wg, wu, wd)
        in_specs = [x_spec, y_spec, y_spec, _const_spec(wo.shape)] + w_specs
        block_bytes += 2 * _nbytes((FFN_ROWS, half), BF16) + _nbytes(wo.shape, BF16)
    return pl.pallas_call(
        body,
        out_shape=jax.ShapeDtypeStruct((t, d), F32),
        grid=(t // FFN_ROWS,),
        in_specs=in_specs,
        out_specs=x_spec,
        scratch_shapes=[pltpu.VMEM((FFN_ROWS, d_ff), F32)],
        compiler_params=pltpu.CompilerParams(
            dimension_semantics=("parallel",), vmem_limit_bytes=_vmem_limit(block_bytes)),
        name="ffn" if mix is None else "mix_ffn",
    )(*operands)


def _segment_mean_sq(x, seg_ref):
    return _dot((x * x).astype(BF16), seg_ref[...])


def _proj_kernel(x_ref, gmix_ref, win_ref, gqa_ref, wuq_ref, gkva_ref, wk_ref, wv_ref,
                 vones_ref, gq_ref, gk_ref, gkr_ref, gsq_ref, gsk_ref,
                 seg_mla_ref, seg_swa_ref, rc_ref, ra_ref, rb_ref,
                 q_ref, k_ref, v_ref, sq_ref, sk_ref, sv_ref):
    pairs = range(MLA_HEADS // 2)
    grp = lambda i: slice(i * 2 * LANES, (i + 1) * 2 * LANES)

    def reduce_stage(rows):
        h = _rms(x_ref[rows, :], gmix_ref[...]).astype(BF16)
        rc, ra, rb = rc_ref[rows, :], ra_ref[rows, :], rb_ref[rows, :]

        def rope(t):
            return t * rc + pltpu.roll(t, LANES - 16, 1) * ra + pltpu.roll(t, 16, 1) * rb

        c_q = _rms(_dot(h, win_ref[:, 0:256]), gqa_ref[...]).astype(BF16)
        c_kv = _rms(_dot(h, win_ref[:, 256:384]), gkva_ref[...]).astype(BF16)
        hkr = _dot(h, win_ref[:, 384:512])
        kr_ms = jnp.sum(hkr * hkr, axis=-1, keepdims=True) * (1.0 / MLA_ROPE)
        k_pe = rope(hkr * lax.rsqrt(kr_ms + RMS_EPS) * gkr_ref[...])
        return dict(rows=rows, h=h, c_q=c_q, c_kv=c_kv, k_pe=k_pe, rope=rope)

    def raw_stage(st):
        h, c_q, c_kv, rows = st["h"], st["c_q"], st["c_kv"], st["rows"]
        st["q_raw"] = [_dot(c_q, wuq_ref[:, grp(i)]) for i in pairs]
        st["k_raw"] = [_dot(c_kv, wk_ref[:, grp(i)]) for i in pairs]
        st["s_raw"] = [_dot(h, win_ref[:, 512 + i * 2 * LANES:512 + (i + 1) * 2 * LANES])
                       for i in range(3)]
        sv_ref[rows, :] = _dot(h, win_ref[:, 1280:1536]).astype(BF16)
        for i in pairs:
            v_ref[rows, grp(i)] = (_dot(c_kv, wv_ref[:, grp(i)]) + vones_ref[:, grp(i)]).astype(BF16)

    def stats_stage(st):
        st["q_ms"] = [_segment_mean_sq(t, seg_mla_ref) for t in st["q_raw"]]
        st["k_ms"] = [_segment_mean_sq(t, seg_mla_ref) for t in st["k_raw"]]
        st["s_ms"] = [_segment_mean_sq(t, seg_swa_ref) for t in st["s_raw"]]

    def store_stage(st):
        rows, rope = st["rows"], st["rope"]
        for i in pairs:
            q = st["q_raw"][i] * lax.rsqrt(st["q_ms"][i] + RMS_EPS) * gq_ref[:, grp(i)]
            k = st["k_raw"][i] * lax.rsqrt(st["k_ms"][i] + RMS_EPS) * gk_ref[:, grp(i)]
            for j in range(2):
                blk = slice(j * LANES, (j + 1) * LANES)
                dst = slice((2 * i + j) * LANES, (2 * i + j + 1) * LANES)
                q_ref[rows, dst] = rope(q[:, blk]).astype(BF16)
                k_ref[rows, dst] = (k[:, blk] + st["k_pe"]).astype(BF16)
        s_raw, s_ms = st["s_raw"], st["s_ms"]
        for i in range(2):
            sq_ref[rows, grp(i)] = (s_raw[i] * lax.rsqrt(s_ms[i] + RMS_EPS) * gsq_ref[:, grp(i)]).astype(BF16)
        sk_ref[rows, :] = (s_raw[2] * lax.rsqrt(s_ms[2] + RMS_EPS) * gsk_ref[...]).astype(BF16)

    chunk = PROJ_ROWS // PROJ_CHUNKS
    stages = (raw_stage, stats_stage, store_stage)
    states = []
    for step in range(PROJ_CHUNKS + len(stages)):
        if step < PROJ_CHUNKS:
            states.append(reduce_stage(slice(step * chunk, (step + 1) * chunk)))
        for age, stage in enumerate(stages):
            c = step - 1 - age
            if 0 <= c < PROJ_CHUNKS:
                stage(states[c])


def _proj_call(x1, batch, seq, consts):
    t, d = x1.shape
    assert seq % PROJ_ROWS == 0
    steps = seq // PROJ_ROWS
    row = lambda j, b: (b * steps + j, 0)
    pos = lambda j, b: (j, 0)
    in_specs = [pl.BlockSpec((PROJ_ROWS, d), row)]
    in_specs += [_const_spec(c.shape) for c in consts[:-3]]
    in_specs += [pl.BlockSpec((PROJ_ROWS, LANES), pos)] * 3
    widths = (MLA_HEADS * LANES, MLA_HEADS * LANES, MLA_HEADS * LANES,
              SWA_HEADS * SWA_HEAD_DIM, 2 * LANES, 2 * LANES)
    out_shape = [jax.ShapeDtypeStruct((t, w), BF16) for w in widths]
    out_specs = [pl.BlockSpec((PROJ_ROWS, w), row) for w in widths]
    block_bytes = (_nbytes((PROJ_ROWS, d), F32) + sum(_nbytes(c.shape, c.dtype) for c in consts[:-3])
                   + 3 * _nbytes((PROJ_ROWS, LANES), F32)
                   + sum(_nbytes((PROJ_ROWS, w), BF16) for w in widths))
    return pl.pallas_call(
        _proj_kernel,
        out_shape=out_shape,
        grid=(steps, batch),
        in_specs=in_specs,
        out_specs=out_specs,
        compiler_params=pltpu.CompilerParams(
            dimension_semantics=("parallel", "parallel"),
            vmem_limit_bytes=_vmem_limit(block_bytes)),
        name="proj",
    )(x1, *consts)


def _mla_stages(q_ref, k_ref, v_ref, o_ref):
    part_rows = q_ref.shape[0] // MLA_ROW_PARTS
    lane = lax.broadcasted_iota(jnp.int32, (part_rows, LANES), 1)
    outs = {}

    def scores(unit):
        part, h = unit
        blk = slice(h * LANES, (h + 1) * LANES)
        rows = slice(part * part_rows, (part + 1) * part_rows)
        return _dot_nt(q_ref[rows, blk], k_ref[:, blk])

    def finish(unit, s):
        part, h = unit
        rows = slice(part * part_rows, (part + 1) * part_rows)
        m = jnp.max(s, axis=-1, keepdims=True)
        o = None
        for c in range(0, s.shape[1], MLA_KEY_CHUNK):
            p = jnp.exp2(s[:, c:c + MLA_KEY_CHUNK] - m).astype(BF16)
            pv = _dot(p, v_ref[c:c + MLA_KEY_CHUNK, h * LANES:(h + 1) * LANES])
            o = pv if o is None else o + pv
        outs[unit] = o / pltpu.roll(o, LANES // 2, 1)
        if h % 2 == 1:
            o_ref[rows, (h // 2) * LANES:(h // 2 + 1) * LANES] = (
                jnp.where(lane < LANES // 2, outs[(part, h - 1)], outs[unit]).astype(BF16))

    return scores, finish


def _mla_kernel(q_ref, k_ref, v_ref, o_ref):
    scores, finish = _mla_stages(q_ref, k_ref, v_ref, o_ref)
    units = [(part, h) for part in range(MLA_ROW_PARTS) for h in range(MLA_HEADS)]
    s = scores(units[0])
    for i, unit in enumerate(units):
        s_next = scores(units[i + 1]) if i + 1 < len(units) else None
        finish(unit, s)
        s = s_next


def _mla_call(q, k, v, batch, seq):
    t, width = q.shape
    assert seq % MLA_ROWS == 0
    steps = seq // MLA_ROWS
    q_spec = pl.BlockSpec((MLA_ROWS, width), lambda b, i: (b * steps + i, 0))
    kv_spec = pl.BlockSpec((seq, width), lambda b, i: (b, 0))
    o_spec = pl.BlockSpec((MLA_ROWS, MLA_HEADS * MLA_V), lambda b, i: (b * steps + i, 0))
    block_bytes = (2 * _nbytes((seq, width), BF16) + 2 * _nbytes((MLA_ROWS, width), BF16)
                   + 4 * _nbytes((MLA_ROWS, seq), F32))
    return pl.pallas_call(
        _mla_kernel,
        out_shape=jax.ShapeDtypeStruct((t, MLA_HEADS * MLA_V), BF16),
        grid=(batch, steps),
        in_specs=[q_spec, kv_spec, kv_spec],
        out_specs=o_spec,
        compiler_params=pltpu.CompilerParams(
            dimension_semantics=("parallel", "parallel"),
            vmem_limit_bytes=_vmem_limit(block_bytes)),
        name="mla_attn",
    )(q, k, v)


def _swa_key_start(i, seq):
    return jnp.clip(i * SWA_Q_ROWS - WINDOW, 0, seq - SWA_KEYS)


def _swa_bias_tables(seq):
    steps = seq // SWA_Q_ROWS
    r = jnp.arange(SWA_Q_ROWS)[:, None]
    c = jnp.arange(SWA_KEYS)[None, :]
    slopes = 2.0 ** (-(8.0 / SWA_HEADS) * jnp.arange(1, SWA_HEADS + 1, dtype=F32))
    cases = []
    for i in (0, 1, steps - 1):
        dist = jnp.abs(c - r + (_swa_key_start(i, seq) - i * SWA_Q_ROWS)).astype(F32)
        bias = -LOG2E * slopes[:, None, None] * dist[None]
        cases.append(jnp.where((dist <= WINDOW)[None], bias, -jnp.inf))
    return jnp.stack(cases)


def _swa_stages(sink_ref, bias_ref, q_ref, k_ref, v_ref, o_ref):
    seq = k_ref.shape[0]
    blocks = seq // SWA_Q_ROWS
    rows = SWA_Q_ROWS
    blocks_per_step = q_ref.shape[0] // rows
    lane = lax.broadcasted_iota(jnp.int32, (rows, LANES), 1)
    low_half = lane < LANES // 2
    kv_lane = lax.broadcasted_iota(jnp.int32, (SWA_KEYS, 2 * LANES), 1)
    low_kv = (kv_lane & (LANES // 2)) == 0

    pairs = range(SWA_GROUP // 2)
    q_block = lambda g, pr: slice((g * (SWA_GROUP // 2) + pr) * LANES,
                                  (g * (SWA_GROUP // 2) + pr + 1) * LANES)

    def score_units(u):
        blk = pl.program_id(1) * blocks_per_step + u
        key0 = pl.multiple_of(_swa_key_start(blk, seq), LANES)
        case = jnp.where(blk == 0, 0, jnp.where(blk == blocks - 1, 2, 1))
        qrows = slice(u * rows, (u + 1) * rows)
        kwin = k_ref[pl.ds(key0, SWA_KEYS), :].astype(F32)
        vwin = v_ref[pl.ds(key0, SWA_KEYS), :].astype(F32)
        k_lo = jnp.where(low_kv, kwin, 0.0).astype(BF16)
        k_hi = jnp.where(low_kv, 0.0, kwin).astype(BF16)
        v_lo = jnp.where(low_kv, vwin, 0.0).astype(BF16)
        v_hi = jnp.where(low_kv, 0.0, vwin).astype(BF16)
        units = []
        for g in range(SWA_KV_HEADS):
            grp = slice(g * LANES, (g + 1) * LANES)
            k_bd = jnp.concatenate([k_lo[:, grp], k_hi[:, grp]], axis=0)
            v_bd = jnp.concatenate([v_lo[:, grp], v_hi[:, grp]], axis=0)
            q2 = jnp.concatenate([q_ref[qrows, q_block(g, pr)] for pr in pairs], axis=0)
            units.append((g, qrows, case, _dot_nt(q2, k_bd), v_bd))
        return units

    def finish(unit):
        g, qrows, case, scores, v_bd = unit
        p_rows, denoms = [], []
        for pr in pairs:
            p_halves, d_halves = [], []
            for half in range(2):
                head = g * SWA_GROUP + 2 * pr + half
                logits = (scores[pr * rows:(pr + 1) * rows, half * SWA_KEYS:(half + 1) * SWA_KEYS]
                          + bias_ref[case, head])
                sink = sink_ref[head] * LOG2E
                m = jnp.maximum(jnp.max(logits, axis=-1, keepdims=True), sink)
                p = jnp.exp2(logits - m)
                d_halves.append(jnp.sum(p, axis=-1, keepdims=True) + jnp.exp2(sink - m))
                p_halves.append(p.astype(BF16))
            p_rows.append(jnp.concatenate(p_halves, axis=1))
            denoms.append(jnp.where(low_half, d_halves[0], d_halves[1]))
        o = _dot(jnp.concatenate(p_rows, axis=0), v_bd)
        for pr in pairs:
            o_ref[qrows, q_block(g, pr)] = (o[pr * rows:(pr + 1) * rows] / denoms[pr]).astype(BF16)

    return score_units, finish


def _swa_kernel(sink_ref, bias_ref, q_ref, k_ref, v_ref, o_ref):
    score_units, finish = _swa_stages(sink_ref, bias_ref, q_ref, k_ref, v_ref, o_ref)
    blocks = q_ref.shape[0] // SWA_Q_ROWS
    pending = score_units(0)
    for u in range(blocks):
        current = pending
        if u + 1 < blocks:
            pending = score_units(u + 1)
        for unit in current:
            finish(unit)


def _swa_call(sink, sq, sk, sv, batch, seq):
    t, width = sq.shape
    assert seq % SWA_ROWS == 0 and SWA_ROWS % SWA_Q_ROWS == 0
    assert seq >= SWA_KEYS and seq // SWA_Q_ROWS >= 3
    steps = seq // SWA_ROWS
    q_spec = pl.BlockSpec((SWA_ROWS, width), lambda b, i: (b * steps + i, 0))
    kv_spec = pl.BlockSpec((seq, 2 * LANES), lambda b, i: (b, 0))
    bias = _swa_bias_tables(seq)
    block_bytes = (2 * _nbytes((seq, 2 * LANES), BF16) + 2 * _nbytes((SWA_ROWS, width), BF16)
                   + _nbytes(bias.shape, F32) + 4 * _nbytes((SWA_ROWS, 2 * SWA_KEYS), F32))
    return pl.pallas_call(
        _swa_kernel,
        out_shape=jax.ShapeDtypeStruct((t, width), BF16),
        grid=(batch, steps),
        in_specs=[pl.BlockSpec(memory_space=pltpu.SMEM), _const_spec(bias.shape),
                  q_spec, kv_spec, kv_spec],
        out_specs=q_spec,
        compiler_params=pltpu.CompilerParams(
            dimension_semantics=("parallel", "parallel"),
            vmem_limit_bytes=_vmem_limit(block_bytes)),
        name="swa_attn",
    )(sink, bias, sq, sk, sv)


def _head_blocks(w, heads, width, lo, hi, at=0):
    rows = w.shape[0]
    part = w.reshape(rows, heads, width)[:, :, lo:hi]
    part = jnp.pad(part, ((0, 0), (0, 0), (at, LANES - at - (hi - lo))))
    return part.reshape(rows, heads * LANES)


def _proj_constants(seq, g_mix, w_in, g_q_a, w_uq, g_kv_a, w_ukv,
                    g_qn, g_qr, g_kn, g_kr, g_swa_q, g_swa_k):
    d = w_in.shape[0]
    q_rank, kv_rank = g_q_a.shape[0], g_kv_a.shape[0]
    o_kr = q_rank + kv_rank
    o_sq = o_kr + MLA_ROPE
    o_sk = o_sq + SWA_HEADS * SWA_HEAD_DIM
    o_sv = o_sk + SWA_KV_HEADS * SWA_HEAD_DIM
    kr_block = jnp.pad(w_in[:, o_kr:o_sq], ((0, 0), (MLA_NOPE, LANES - MLA_NOPE - MLA_ROPE)))
    dup = lambda w: jnp.repeat(w.reshape(d, SWA_KV_HEADS, 1, SWA_HEAD_DIM), 2, axis=2).reshape(d, -1)
    win = jnp.concatenate(
        [w_in[:, :o_kr], kr_block, w_in[:, o_sq:o_sk], dup(w_in[:, o_sk:o_sv]), dup(w_in[:, o_sv:])],
        axis=1).astype(BF16)

    qk_width = MLA_NOPE + MLA_ROPE
    wuq = _head_blocks(w_uq, MLA_HEADS, qk_width, 0, qk_width).astype(BF16)
    kv_width = MLA_NOPE + MLA_V
    wk = _head_blocks(w_ukv, MLA_HEADS, kv_width, 0, MLA_NOPE).astype(BF16)
    wv_even = _head_blocks(w_ukv, MLA_HEADS, kv_width, MLA_NOPE, kv_width, at=0)
    wv_odd = _head_blocks(w_ukv, MLA_HEADS, kv_width, MLA_NOPE, kv_width, at=LANES // 2)
    head_of_col = jnp.arange(MLA_HEADS * LANES) // LANES
    odd_col = (head_of_col % 2 == 1)[None, :]
    wv = jnp.where(odd_col, wv_odd, wv_even).astype(BF16)
    lane = jnp.arange(MLA_HEADS * LANES) % LANES
    vones = ((lane >= LANES // 2) != odd_col[0]).astype(F32)[None, :]

    scale = LOG2E / math.sqrt(qk_width)
    zeros = lambda n: jnp.zeros((n,), F32)
    gq = jnp.tile(jnp.concatenate([g_qn, g_qr, zeros(LANES - qk_width)]) * scale, MLA_HEADS)[None, :]
    gk = jnp.tile(jnp.concatenate([g_kn, zeros(LANES - MLA_NOPE)]), MLA_HEADS)[None, :]
    gkr = jnp.concatenate([zeros(MLA_NOPE), g_kr, zeros(LANES - qk_width)])[None, :]
    gsq = jnp.tile(g_swa_q * (LOG2E / math.sqrt(SWA_HEAD_DIM)), SWA_HEADS)[None, :]
    gsk = jnp.tile(g_swa_k, 2 * SWA_KV_HEADS)[None, :]

    idx = jnp.arange(2 * LANES)
    in_blk = idx % LANES
    seg_id = jnp.where(in_blk < MLA_NOPE, 0, jnp.where(in_blk < qk_width, 1, 2)) + 3 * (idx // LANES)
    seg_len = jnp.where(in_blk < MLA_NOPE, MLA_NOPE, MLA_ROPE).astype(F32)
    same = (seg_id[:, None] == seg_id[None, :]) & (in_blk < qk_width)[:, None]
    seg_mla = jnp.where(same, 1.0 / seg_len[None, :], 0.0).astype(BF16)
    seg_swa = jnp.where((idx // SWA_HEAD_DIM)[:, None] == (idx // SWA_HEAD_DIM)[None, :],
                        1.0 / SWA_HEAD_DIM, 0.0).astype(BF16)

    pos = jnp.arange(seq, dtype=F32)
    inv = 1.0 / (ROPE_THETA ** (jnp.arange(0, MLA_ROPE, 2, dtype=F32) / MLA_ROPE))
    ang = pos[:, None] * inv[None, :]
    cos, sin = jnp.cos(ang), jnp.sin(ang)
    half = MLA_ROPE // 2
    pad_lo = jnp.ones((seq, MLA_NOPE), F32)
    pad_hi = jnp.ones((seq, LANES - qk_width), F32)
    rc = jnp.concatenate([pad_lo, cos, cos, pad_hi], axis=1)
    ra = jnp.concatenate([0 * pad_lo, -sin, jnp.zeros((seq, half), F32), 0 * pad_hi], axis=1)
    rb = jnp.concatenate([0 * pad_lo, jnp.zeros((seq, half), F32), sin, 0 * pad_hi], axis=1)

    row = lambda g: g.astype(F32)[None, :]
    return (row(g_mix), win, row(g_q_a), wuq, row(g_kv_a), wk, wv, vones,
            gq, gk, gkr, gsq, gsk, seg_mla, seg_swa, rc, ra, rb)


def kernel(x, g_ffn1, w1_gate, w1_up, w1_down, g_mix, w_in, g_q_a, w_uq, g_kv_a, w_ukv,
           g_mla_qn, g_mla_qr, g_mla_kn, g_mla_kr, g_swa_q, g_swa_k, sink, w_o,
           g_ffn2, w2_gate, w2_up, w2_down):
    batch, seq, d = x.shape
    x2d = x.reshape(batch * seq, d)
    bf = lambda w: w.astype(BF16)
    row = lambda g: g.astype(F32)[None, :]

    x1 = _ffn_call(x2d, row(g_ffn1), w1_gate, w1_up, w1_down)
    consts = _proj_constants(seq, g_mix, w_in, g_q_a, w_uq, g_kv_a, w_ukv,
                             g_mla_qn, g_mla_qr, g_mla_kn, g_mla_kr, g_swa_q, g_swa_k)
    q, k, v, sq, sk, sv = _proj_call(x1, batch, seq, consts)
    y_a = _mla_call(q, k, v, batch, seq)
    y_b = _swa_call(sink.astype(F32), sq, sk, sv, batch, seq)
    out = _ffn_call(x1, row(g_ffn2), w2_gate, w2_up, w2_down,
                    mix=(y_a, y_b, bf(w_o)))
    return out.reshape(batch, seq, d)
```

```python
import math

import numpy as np
import jax
import jax.numpy as jnp
from jax import lax
from jax.experimental import pallas as pl
from jax.experimental.pallas import tpu as pltpu

F32 = jnp.float32
BF16 = jnp.bfloat16

RMS_EPS = 1e-6
MLA_HEADS = 8
MLA_NOPE = 64
MLA_ROPE = 32
MLA_V = 64
ROPE_THETA = 10000.0
SWA_HEADS = 8
SWA_KV_HEADS = 2
SWA_HEAD_DIM = 64
SWA_GROUP = SWA_HEADS // SWA_KV_HEADS
WINDOW = 128
LOG2E = math.log2(math.e)

LANES = 128
V7X_VMEM_BYTES = 64 * 1024 * 1024
VMEM_HEADROOM_BYTES = 8 * 1024 * 1024

FFN_ROWS = 512
FFN_CHUNK = 256
PROJ_ROWS = 1024
MLA_ROWS = 1024
MLA_KEY_CHUNK = 512
SWA_ROWS = 1024
SWA_Q_ROWS = 128
SWA_KEYS = SWA_Q_ROWS + 2 * WINDOW


def _vmem_limit(block_bytes):
    want = 2 * block_bytes + 16 * 1024 * 1024
    return int(min(want, V7X_VMEM_BYTES - VMEM_HEADROOM_BYTES))


def _nbytes(shape, dtype):
    return math.prod(shape) * jnp.dtype(dtype).itemsize


def _const_spec(shape):
    zeros = (0,) * len(shape)
    return pl.BlockSpec(shape, lambda *_: zeros, pipeline_mode=pl.Buffered(1))


def _rms(x, gain):
    ms = jnp.mean(x * x, axis=-1, keepdims=True)
    return x * lax.rsqrt(ms + RMS_EPS) * gain


def _dot(a, b):
    return jnp.dot(a, b, preferred_element_type=F32)


def _dot_nt(a, b):
    return lax.dot_general(a, b, (((1,), (1,)), ((), ())), preferred_element_type=F32)


def _swiglu_residual(x, gain_ref, wg_ref, wu_ref, wd_ref, act_ref):
    h = _rms(x, gain_ref[...])
    d_ff = wg_ref.shape[1]
    for c in range(d_ff // FFN_CHUNK):
        cols = slice(c * FFN_CHUNK, (c + 1) * FFN_CHUNK)
        gate = _dot(h, wg_ref[:, cols])
        up = _dot(h, wu_ref[:, cols])
        act_ref[:, cols] = gate / (1.0 + jnp.exp(-gate)) * up
    return x + 0.5 * _dot(act_ref[...], wd_ref[...])


def _ffn_kernel(x_ref, gain_ref, wg_ref, wu_ref, wd_ref, o_ref, act_ref):
    o_ref[...] = _swiglu_residual(x_ref[...], gain_ref, wg_ref, wu_ref, wd_ref, act_ref)


def _mix_ffn_kernel(x_ref, ya_ref, yb_ref, wo_ref, gain_ref, wg_ref, wu_ref, wd_ref,
                    o_ref, act_ref):
    half = ya_ref.shape[1]
    x2 = (x_ref[...] + _dot(ya_ref[...], wo_ref[:half, :])
          + _dot(yb_ref[...], wo_ref[half:, :]))
    o_ref[...] = _swiglu_residual(x2, gain_ref, wg_ref, wu_ref, wd_ref, act_ref)


def _ffn_call(x, gain, wg, wu, wd, mix=None):
    t, d = x.shape
    d_ff = wg.shape[1]
    assert t % FFN_ROWS == 0 and d_ff % FFN_CHUNK == 0
    row = lambda i: (i, 0)
    x_spec = pl.BlockSpec((FFN_ROWS, d), row)
    w_specs = [_const_spec((1, d)), _const_spec((d, d_ff)), _const_spec((d, d_ff)),
               _const_spec((d_ff, d))]
    block_bytes = (2 * _nbytes((FFN_ROWS, d), F32) + 3 * _nbytes((d, d_ff), wg.dtype)
                   + _nbytes((FFN_ROWS, d_ff), F32))
    if mix is None:
        body, operands, in_specs = _ffn_kernel, (x, gain, wg, wu, wd), [x_spec] + w_specs
    else:
        ya, yb, wo = mix
        half = ya.shape[1]
        y_spec = pl.BlockSpec((FFN_ROWS, half), row)
        body = _mix_ffn_kernel
        operands = (x, ya, yb, wo, gain, wg, wu, wd)
        in_specs = [x_spec, y_spec, y_spec, _const_spec(wo.shape)] + w_specs
        block_bytes += 2 * _nbytes((FFN_ROWS, half), BF16) + _nbytes(wo.shape, BF16)
    return pl.pallas_call(
        body,
        out_shape=jax.ShapeDtypeStruct((t, d), F32),
        grid=(t // FFN_ROWS,),
        in_specs=in_specs,
        out_specs=x_spec,
        scratch_shapes=[pltpu.VMEM((FFN_ROWS, d_ff), F32)],
        compiler_params=pltpu.CompilerParams(
            dimension_semantics=("parallel",), vmem_limit_bytes=_vmem_limit(block_bytes)),
        name="ffn" if mix is None else "mix_ffn",
    )(*operands)


def _segment_mean_sq(x, seg_ref):
    return _dot((x * x).astype(BF16), seg_ref[...])


def _proj_kernel(x_ref, gmix_ref, win_ref, gqa_ref, wuq_ref, gkva_ref, wk_ref, wv_ref,
                 vones_ref, gq_ref, gk_ref, gkr_ref, gsq_ref, gsk_ref,
                 seg_mla_ref, seg_swa_ref, rc_ref, ra_ref, rb_ref,
                 q_ref, k_ref, v_ref, sq_ref, sk_ref, sv_ref):
    pairs = range(MLA_HEADS // 2)
    grp = lambda i: slice(i * 2 * LANES, (i + 1) * 2 * LANES)

    def reduce_stage(rows):
        h = _rms(x_ref[rows, :], gmix_ref[...]).astype(BF16)
        rc, ra, rb = rc_ref[rows, :], ra_ref[rows, :], rb_ref[rows, :]

        def rope(t):
            return t * rc + pltpu.roll(t, LANES - 16, 1) * ra + pltpu.roll(t, 16, 1) * rb

        c_q = _rms(_dot(h, win_ref[:, 0:256]), gqa_ref[...]).astype(BF16)
        c_kv = _rms(_dot(h, win_ref[:, 256:384]), gkva_ref[...]).astype(BF16)
        hkr = _dot(h, win_ref[:, 384:512])
        kr_ms = jnp.sum(hkr * hkr, axis=-1, keepdims=True) * (1.0 / MLA_ROPE)
        k_pe = rope(hkr * lax.rsqrt(kr_ms + RMS_EPS) * gkr_ref[...])
        return dict(rows=rows, h=h, c_q=c_q, c_kv=c_kv, k_pe=k_pe, rope=rope)

    def raw_stage(st):
        h, c_q, c_kv, rows = st["h"], st["c_q"], st["c_kv"], st["rows"]
        st["q_raw"] = [_dot(c_q, wuq_ref[:, grp(i)]) for i in pairs]
        st["k_raw"] = [_dot(c_kv, wk_ref[:, grp(i)]) for i in pairs]
        st["s_raw"] = [_dot(h, win_ref[:, 512 + i * 2 * LANES:512 + (i + 1) * 2 * LANES])
                       for i in range(3)]
        sv_ref[rows, :] = _dot(h, win_ref[:, 1280:1536]).astype(BF16)
        for i in pairs:
            v_ref[rows, grp(i)] = (_dot(c_kv, wv_ref[:, grp(i)]) + vones_ref[:, grp(i)]).astype(BF16)

    def stats_stage(st):
        st["q_ms"] = [_segment_mean_sq(t, seg_mla_ref) for t in st["q_raw"]]
        st["k_ms"] = [_segment_mean_sq(t, seg_mla_ref) for t in st["k_raw"]]
        st["s_ms"] = [_segment_mean_sq(t, seg_swa_ref) for t in st["s_raw"]]

    def store_stage(st):
        rows, rope = st["rows"], st["rope"]
        for i in pairs:
            q = st["q_raw"][i] * lax.rsqrt(st["q_ms"][i] + RMS_EPS) * gq_ref[:, grp(i)]
            k = st["k_raw"][i] * lax.rsqrt(st["k_ms"][i] + RMS_EPS) * gk_ref[:, grp(i)]
            for j in range(2):
                blk = slice(j * LANES, (j + 1) * LANES)
                dst = slice((2 * i + j) * LANES, (2 * i + j + 1) * LANES)
                q_ref[rows, dst] = rope(q[:, blk]).astype(BF16)
                k_ref[rows, dst] = (k[:, blk] + st["k_pe"]).astype(BF16)
        s_raw, s_ms = st["s_raw"], st["s_ms"]
        for i in range(2):
            sq_ref[rows, grp(i)] = (s_raw[i] * lax.rsqrt(s_ms[i] + RMS_EPS) * gsq_ref[:, grp(i)]).astype(BF16)
        sk_ref[rows, :] = (s_raw[2] * lax.rsqrt(s_ms[2] + RMS_EPS) * gsk_ref[...]).astype(BF16)

    state = reduce_stage(slice(None))
    raw_stage(state)
    stats_stage(state)
    store_stage(state)


def _proj_call(x1, batch, seq, consts):
    t, d = x1.shape
    assert seq % PROJ_ROWS == 0
    steps = seq // PROJ_ROWS
    row = lambda j, b: (b * steps + j, 0)
    pos = lambda j, b: (j, 0)
    in_specs = [pl.BlockSpec((PROJ_ROWS, d), row)]
    in_specs += [_const_spec(c.shape) for c in consts[:-3]]
    in_specs += [pl.BlockSpec((PROJ_ROWS, LANES), pos)] * 3
    widths = (MLA_HEADS * LANES, MLA_HEADS * LANES, MLA_HEADS * LANES,
              SWA_HEADS * SWA_HEAD_DIM, 2 * LANES, 2 * LANES)
    out_shape = [jax.ShapeDtypeStruct((t, w), BF16) for w in widths]
    out_specs = [pl.BlockSpec((PROJ_ROWS, w), row) for w in widths]
    block_bytes = (_nbytes((PROJ_ROWS, d), F32) + sum(_nbytes(c.shape, c.dtype) for c in consts[:-3])
                   + 3 * _nbytes((PROJ_ROWS, LANES), F32)
                   + sum(_nbytes((PROJ_ROWS, w), BF16) for w in widths))
    return pl.pallas_call(
        _proj_kernel,
        out_shape=out_shape,
        grid=(steps, batch),
        in_specs=in_specs,
        out_specs=out_specs,
        compiler_params=pltpu.CompilerParams(
            dimension_semantics=("parallel", "parallel"),
            vmem_limit_bytes=_vmem_limit(block_bytes)),
        name="proj",
    )(x1, *consts)


def _mla_kernel(q_ref, k_ref, v_ref, o_ref):
    lane = lax.broadcasted_iota(jnp.int32, (q_ref.shape[0], LANES), 1)
    outs = {}

    def scores(h):
        blk = slice(h * LANES, (h + 1) * LANES)
        return _dot_nt(q_ref[:, blk], k_ref[:, blk])

    def finish(h, s):
        m = jnp.max(s, axis=-1, keepdims=True)
        o = None
        for c in range(0, s.shape[1], MLA_KEY_CHUNK):
            p = jnp.exp2(s[:, c:c + MLA_KEY_CHUNK] - m).astype(BF16)
            pv = _dot(p, v_ref[c:c + MLA_KEY_CHUNK, h * LANES:(h + 1) * LANES])
            o = pv if o is None else o + pv
        outs[h] = o / pltpu.roll(o, LANES // 2, 1)
        if h % 2 == 1:
            o_ref[:, (h // 2) * LANES:(h // 2 + 1) * LANES] = (
                jnp.where(lane < LANES // 2, outs[h - 1], outs[h]).astype(BF16))

    s = scores(0)
    for h in range(MLA_HEADS):
        s_next = scores(h + 1) if h + 1 < MLA_HEADS else None
        finish(h, s)
        s = s_next


def _mla_call(q, k, v, batch, seq):
    t, width = q.shape
    assert seq % MLA_ROWS == 0
    steps = seq // MLA_ROWS
    q_spec = pl.BlockSpec((MLA_ROWS, width), lambda b, i: (b * steps + i, 0))
    kv_spec = pl.BlockSpec((seq, width), lambda b, i: (b, 0))
    o_spec = pl.BlockSpec((MLA_ROWS, MLA_HEADS * MLA_V), lambda b, i: (b * steps + i, 0))
    block_bytes = (2 * _nbytes((seq, width), BF16) + 2 * _nbytes((MLA_ROWS, width), BF16)
                   + 4 * _nbytes((MLA_ROWS, seq), F32))
    return pl.pallas_call(
        _mla_kernel,
        out_shape=jax.ShapeDtypeStruct((t, MLA_HEADS * MLA_V), BF16),
        grid=(batch, steps),
        in_specs=[q_spec, kv_spec, kv_spec],
        out_specs=o_spec,
        compiler_params=pltpu.CompilerParams(
            dimension_semantics=("parallel", "parallel"),
            vmem_limit_bytes=_vmem_limit(block_bytes)),
        name="mla_attn",
    )(q, k, v)


def _swa_key_start(i, seq):
    return jnp.clip(i * SWA_Q_ROWS - WINDOW, 0, seq - SWA_KEYS)


def _swa_bias_tables(seq):
    steps = seq // SWA_Q_ROWS
    r = np.arange(SWA_Q_ROWS)[:, None]
    c = np.arange(SWA_KEYS)[None, :]
    slopes = 2.0 ** (-(8.0 / SWA_HEADS) * np.arange(1, SWA_HEADS + 1))
    cases = []
    for i in (0, 1, steps - 1):
        key0 = min(max(i * SWA_Q_ROWS - WINDOW, 0), seq - SWA_KEYS)
        dist = np.abs(c - r + (key0 - i * SWA_Q_ROWS)).astype(np.float64)
        bias = -LOG2E * slopes[:, None, None] * dist[None]
        cases.append(np.where((dist <= WINDOW)[None], bias, -np.inf))
    return np.stack(cases).astype(np.float32)


def _swa_stages(sink_ref, bias_ref, q_ref, k_ref, v_ref, o_ref):
    seq = k_ref.shape[0]
    blocks = seq // SWA_Q_ROWS
    rows = SWA_Q_ROWS
    blocks_per_step = q_ref.shape[0] // rows
    lane = lax.broadcasted_iota(jnp.int32, (rows, LANES), 1)
    low_half = lane < LANES // 2
    kv_lane = lax.broadcasted_iota(jnp.int32, (SWA_KEYS, 2 * LANES), 1)
    low_kv = (kv_lane & (LANES // 2)) == 0

    pairs = range(SWA_GROUP // 2)
    q_block = lambda g, pr: slice((g * (SWA_GROUP // 2) + pr) * LANES,
                                  (g * (SWA_GROUP // 2) + pr + 1) * LANES)

    def score_units(u):
        blk = pl.program_id(1) * blocks_per_step + u
        key0 = pl.multiple_of(_swa_key_start(blk, seq), LANES)
        case = jnp.where(blk == 0, 0, jnp.where(blk == blocks - 1, 2, 1))
        qrows = slice(u * rows, (u + 1) * rows)
        kwin = k_ref[pl.ds(key0, SWA_KEYS), :].astype(F32)
        vwin = v_ref[pl.ds(key0, SWA_KEYS), :].astype(F32)
        k_lo = jnp.where(low_kv, kwin, 0.0).astype(BF16)
        k_hi = jnp.where(low_kv, 0.0, kwin).astype(BF16)
        v_lo = jnp.where(low_kv, vwin, 0.0).astype(BF16)
        v_hi = jnp.where(low_kv, 0.0, vwin).astype(BF16)
        units = []
        for g in range(SWA_KV_HEADS):
            grp = slice(g * LANES, (g + 1) * LANES)
            k_bd = jnp.concatenate([k_lo[:, grp], k_hi[:, grp]], axis=0)
            v_bd = jnp.concatenate([v_lo[:, grp], v_hi[:, grp]], axis=0)
            q2 = jnp.concatenate([q_ref[qrows, q_block(g, pr)] for pr in pairs], axis=0)
            units.append((g, qrows, case, _dot_nt(q2, k_bd), v_bd))
        return units

    def finish(unit):
        g, qrows, case, scores, v_bd = unit
        p_rows, denoms = [], []
        for pr in pairs:
            p_halves, d_halves = [], []
            for half in range(2):
                head = g * SWA_GROUP + 2 * pr + half
                logits = (scores[pr * rows:(pr + 1) * rows, half * SWA_KEYS:(half + 1) * SWA_KEYS]
                          + bias_ref[case, head])
                sink = sink_ref[head] * LOG2E
                m = jnp.maximum(jnp.max(logits, axis=-1, keepdims=True), sink)
                p = jnp.exp2(logits - m)
                d_halves.append(jnp.sum(p, axis=-1, keepdims=True) + jnp.exp2(sink - m))
                p_halves.append(p.astype(BF16))
            p_rows.append(jnp.concatenate(p_halves, axis=1))
            denoms.append(jnp.where(low_half, d_halves[0], d_halves[1]))
        o = _dot(jnp.concatenate(p_rows, axis=0), v_bd)
        for pr in pairs:
            o_ref[qrows, q_block(g, pr)] = (o[pr * rows:(pr + 1) * rows] / denoms[pr]).astype(BF16)

    return score_units, finish


def _swa_kernel(sink_ref, bias_ref, q_ref, k_ref, v_ref, o_ref):
    score_units, finish = _swa_stages(sink_ref, bias_ref, q_ref, k_ref, v_ref, o_ref)
    blocks = q_ref.shape[0] // SWA_Q_ROWS
    pending = score_units(0)
    for u in range(blocks):
        current = pending
        if u + 1 < blocks:
            pending = score_units(u + 1)
        for unit in current:
            finish(unit)


def _swa_call(sink, sq, sk, sv, batch, seq):
    t, width = sq.shape
    assert seq % SWA_ROWS == 0 and SWA_ROWS % SWA_Q_ROWS == 0
    assert seq >= SWA_KEYS and seq // SWA_Q_ROWS >= 3
    steps = seq // SWA_ROWS
    q_spec = pl.BlockSpec((SWA_ROWS, width), lambda b, i: (b * steps + i, 0))
    kv_spec = pl.BlockSpec((seq, sk.shape[1]), lambda b, i: (b, 0))
    bias = _swa_bias_tables(seq)
    block_bytes = (2 * _nbytes((seq, sk.shape[1]), BF16) + 2 * _nbytes((SWA_ROWS, width), BF16)
                   + _nbytes(bias.shape, F32) + 4 * _nbytes((SWA_ROWS, 2 * SWA_KEYS), F32))
    return pl.pallas_call(
        _swa_kernel,
        out_shape=jax.ShapeDtypeStruct((t, width), BF16),
        grid=(batch, steps),
        in_specs=[pl.BlockSpec(memory_space=pltpu.SMEM), _const_spec(bias.shape),
                  q_spec, kv_spec, kv_spec],
        out_specs=q_spec,
        compiler_params=pltpu.CompilerParams(
            dimension_semantics=("parallel", "parallel"),
            vmem_limit_bytes=_vmem_limit(block_bytes)),
        name="swa_attn",
    )(sink, bias, sq, sk, sv)


def _head_blocks(w, heads, width, lo, hi, at=0):
    rows = w.shape[0]
    part = w.reshape(rows, heads, width)[:, :, lo:hi]
    part = jnp.pad(part, ((0, 0), (0, 0), (at, LANES - at - (hi - lo))))
    return part.reshape(rows, heads * LANES)


def _proj_constants(seq, g_mix, w_in, g_q_a, w_uq, g_kv_a, w_ukv,
                    g_qn, g_qr, g_kn, g_kr, g_swa_q, g_swa_k):
    d = w_in.shape[0]
    q_rank, kv_rank = g_q_a.shape[0], g_kv_a.shape[0]
    o_kr = q_rank + kv_rank
    o_sq = o_kr + MLA_ROPE
    o_sk = o_sq + SWA_HEADS * SWA_HEAD_DIM
    o_sv = o_sk + SWA_KV_HEADS * SWA_HEAD_DIM
    kr_block = jnp.pad(w_in[:, o_kr:o_sq], ((0, 0), (MLA_NOPE, LANES - MLA_NOPE - MLA_ROPE)))
    dup = lambda w: jnp.repeat(w.reshape(d, SWA_KV_HEADS, 1, SWA_HEAD_DIM), 2, axis=2).reshape(d, -1)
    win = jnp.concatenate(
        [w_in[:, :o_kr], kr_block, w_in[:, o_sq:o_sk], dup(w_in[:, o_sk:o_sv]), dup(w_in[:, o_sv:])],
        axis=1).astype(BF16)

    qk_width = MLA_NOPE + MLA_ROPE
    wuq = _head_blocks(w_uq, MLA_HEADS, qk_width, 0, qk_width).astype(BF16)
    kv_width = MLA_NOPE + MLA_V
    wk = _head_blocks(w_ukv, MLA_HEADS, kv_width, 0, MLA_NOPE).astype(BF16)
    wv_even = _head_blocks(w_ukv, MLA_HEADS, kv_width, MLA_NOPE, kv_width, at=0)
    wv_odd = _head_blocks(w_ukv, MLA_HEADS, kv_width, MLA_NOPE, kv_width, at=LANES // 2)
    head_of_col = jnp.arange(MLA_HEADS * LANES) // LANES
    odd_col = (head_of_col % 2 == 1)[None, :]
    wv = jnp.where(odd_col, wv_odd, wv_even).astype(BF16)
    lane = jnp.arange(MLA_HEADS * LANES) % LANES
    vones = ((lane >= LANES // 2) != odd_col[0]).astype(F32)[None, :]

    scale = LOG2E / math.sqrt(qk_width)
    zeros = lambda n: jnp.zeros((n,), F32)
    gq = jnp.tile(jnp.concatenate([g_qn, g_qr, zeros(LANES - qk_width)]) * scale, MLA_HEADS)[None, :]
    gk = jnp.tile(jnp.concatenate([g_kn, zeros(LANES - MLA_NOPE)]), MLA_HEADS)[None, :]
    gkr = jnp.concatenate([zeros(MLA_NOPE), g_kr, zeros(LANES - qk_width)])[None, :]
    gsq = jnp.tile(g_swa_q * (LOG2E / math.sqrt(SWA_HEAD_DIM)), SWA_HEADS)[None, :]
    gsk = jnp.tile(g_swa_k, 2 * SWA_KV_HEADS)[None, :]

    idx = jnp.arange(2 * LANES)
    in_blk = idx % LANES
    seg_id = jnp.where(in_blk < MLA_NOPE, 0, jnp.where(in_blk < qk_width, 1, 2)) + 3 * (idx // LANES)
    seg_len = jnp.where(in_blk < MLA_NOPE, MLA_NOPE, MLA_ROPE).astype(F32)
    same = (seg_id[:, None] == seg_id[None, :]) & (in_blk < qk_width)[:, None]
    seg_mla = jnp.where(same, 1.0 / seg_len[None, :], 0.0).astype(BF16)
    seg_swa = jnp.where((idx // SWA_HEAD_DIM)[:, None] == (idx // SWA_HEAD_DIM)[None, :],
                        1.0 / SWA_HEAD_DIM, 0.0).astype(BF16)

    pos = np.arange(seq, dtype=np.float64)
    inv = 1.0 / (ROPE_THETA ** (np.arange(0, MLA_ROPE, 2, dtype=np.float64) / MLA_ROPE))
    ang = pos[:, None] * inv[None, :]
    cos, sin = np.cos(ang), np.sin(ang)
    half = MLA_ROPE // 2
    pad_lo = np.ones((seq, MLA_NOPE))
    pad_hi = np.ones((seq, LANES - qk_width))
    zero_half = np.zeros((seq, half))
    rc = np.concatenate([pad_lo, cos, cos, pad_hi], axis=1).astype(np.float32)
    ra = np.concatenate([0 * pad_lo, -sin, zero_half, 0 * pad_hi], axis=1).astype(np.float32)
    rb = np.concatenate([0 * pad_lo, zero_half, sin, 0 * pad_hi], axis=1).astype(np.float32)

    row = lambda g: g.astype(F32)[None, :]
    return (row(g_mix), win, row(g_q_a), wuq, row(g_kv_a), wk, wv, vones,
            gq, gk, gkr, gsq, gsk, seg_mla, seg_swa, rc, ra, rb)


def kernel(x, g_ffn1, w1_gate, w1_up, w1_down, g_mix, w_in, g_q_a, w_uq, g_kv_a, w_ukv,
           g_mla_qn, g_mla_qr, g_mla_kn, g_mla_kr, g_swa_q, g_swa_k, sink, w_o,
           g_ffn2, w2_gate, w2_up, w2_down):
    batch, seq, d = x.shape
    x2d = x.reshape(batch * seq, d)
    bf = lambda w: w.astype(BF16)
    row = lambda g: g.astype(F32)[None, :]

    x1 = _ffn_call(x2d, row(g_ffn1), w1_gate, w1_up, w1_down)
    consts = _proj_constants(seq, g_mix, w_in, g_q_a, w_uq, g_kv_a, w_ukv,
                             g_mla_qn, g_mla_qr, g_mla_kn, g_mla_kr, g_swa_q, g_swa_k)
    q, k, v, sq, sk, sv = _proj_call(x1, batch, seq, consts)
    y_a = _mla_call(q, k, v, batch, seq)
    y_b = _swa_call(sink.astype(F32), sq, sk, sv, batch, seq)
    out = _ffn_call(x1, row(g_ffn2), w2_gate, w2_up, w2_down,
                    mix=(y_a, y_b, bf(w_o)))
    return out.reshape(batch, seq, d)
```

```python
import math

import numpy as np
import jax
import jax.numpy as jnp
from jax import lax
from jax.experimental import pallas as pl
from jax.experimental.pallas import tpu as pltpu

F32 = jnp.float32
BF16 = jnp.bfloat16

RMS_EPS = 1e-6
MLA_HEADS = 8
MLA_NOPE = 64
MLA_ROPE = 32
MLA_V = 64
ROPE_THETA = 10000.0
SWA_HEADS = 8
SWA_KV_HEADS = 2
SWA_HEAD_DIM = 64
SWA_GROUP = SWA_HEADS // SWA_KV_HEADS
WINDOW = 128
LOG2E = math.log2(math.e)

LANES = 128
V7X_VMEM_BYTES = 64 * 1024 * 1024
VMEM_HEADROOM_BYTES = 8 * 1024 * 1024

FFN_ROWS = 512
FFN_CHUNK = 256
PROJ_ROWS = 1024
MLA_ROWS = 1024
MLA_KEY_CHUNK = 512
SWA_ROWS = 1024
SWA_Q_ROWS = 128
SWA_KEYS = SWA_Q_ROWS + 2 * WINDOW


def _vmem_limit(block_bytes):
    want = 2 * block_bytes + 16 * 1024 * 1024
    return int(min(want, V7X_VMEM_BYTES - VMEM_HEADROOM_BYTES))


def _nbytes(shape, dtype):
    return math.prod(shape) * jnp.dtype(dtype).itemsize


def _const_spec(shape):
    zeros = (0,) * len(shape)
    return pl.BlockSpec(shape, lambda *_: zeros, pipeline_mode=pl.Buffered(1))


def _rms(x, gain):
    ms = jnp.mean(x * x, axis=-1, keepdims=True)
    return x * lax.rsqrt(ms + RMS_EPS) * gain


def _dot(a, b):
    return jnp.dot(a, b, preferred_element_type=F32)


def _dot_nt(a, b):
    return lax.dot_general(a, b, (((1,), (1,)), ((), ())), preferred_element_type=F32)


def _swiglu_residual(x, gain_ref, wg_ref, wu_ref, wd_ref, act_ref):
    h = _rms(x, gain_ref[...])
    d_ff = wg_ref.shape[1]
    for c in range(d_ff // FFN_CHUNK):
        cols = slice(c * FFN_CHUNK, (c + 1) * FFN_CHUNK)
        gate = _dot(h, wg_ref[:, cols])
        up = _dot(h, wu_ref[:, cols])
        act_ref[:, cols] = gate / (1.0 + jnp.exp(-gate)) * up
    return x + 0.5 * _dot(act_ref[...], wd_ref[...])


def _ffn_kernel(x_ref, gain_ref, wg_ref, wu_ref, wd_ref, o_ref, act_ref):
    o_ref[...] = _swiglu_residual(x_ref[...], gain_ref, wg_ref, wu_ref, wd_ref, act_ref)


def _mix_ffn_kernel(x_ref, ya_ref, yb_ref, wo_ref, gain_ref, wg_ref, wu_ref, wd_ref,
                    o_ref, act_ref):
    half = ya_ref.shape[1]
    x2 = (x_ref[...] + _dot(ya_ref[...], wo_ref[:half, :])
          + _dot(yb_ref[...], wo_ref[half:, :]))
    o_ref[...] = _swiglu_residual(x2, gain_ref, wg_ref, wu_ref, wd_ref, act_ref)


def _ffn_call(x, gain, wg, wu, wd, mix=None):
    t, d = x.shape
    d_ff = wg.shape[1]
    assert t % FFN_ROWS == 0 and d_ff % FFN_CHUNK == 0
    row = lambda i: (i, 0)
    x_spec = pl.BlockSpec((FFN_ROWS, d), row)
    w_specs = [_const_spec((1, d)), _const_spec((d, d_ff)), _const_spec((d, d_ff)),
               _const_spec((d_ff, d))]
    block_bytes = (2 * _nbytes((FFN_ROWS, d), F32) + 3 * _nbytes((d, d_ff), wg.dtype)
                   + _nbytes((FFN_ROWS, d_ff), F32))
    if mix is None:
        body, operands, in_specs = _ffn_kernel, (x, gain, wg, wu, wd), [x_spec] + w_specs
    else:
        ya, yb, wo = mix
        half = ya.shape[1]
        y_spec = pl.BlockSpec((FFN_ROWS, half), row)
        body = _mix_ffn_kernel
        operands = (x, ya, yb, wo, gain, wg, wu, wd)
        in_specs = [x_spec, y_spec, y_spec, _const_spec(wo.shape)] + w_specs
        block_bytes += 2 * _nbytes((FFN_ROWS, half), BF16) + _nbytes(wo.shape, BF16)
    return pl.pallas_call(
        body,
        out_shape=jax.ShapeDtypeStruct((t, d), F32),
        grid=(t // FFN_ROWS,),
        in_specs=in_specs,
        out_specs=x_spec,
        scratch_shapes=[pltpu.VMEM((FFN_ROWS, d_ff), F32)],
        compiler_params=pltpu.CompilerParams(
            dimension_semantics=("parallel",), vmem_limit_bytes=_vmem_limit(block_bytes)),
        name="ffn" if mix is None else "mix_ffn",
    )(*operands)


def _segment_mean_sq(x, seg_ref):
    return _dot((x * x).astype(BF16), seg_ref[...])


def _proj_kernel(x_ref, gmix_ref, win_ref, gqa_ref, wuq_ref, gkva_ref, wk_ref, wv_ref,
                 vones_ref, gq_ref, gk_ref, gkr_ref, gsq_ref, gsk_ref,
                 seg_mla_ref, seg_swa_ref, rc_ref, ra_ref, rb_ref,
                 q_ref, k_ref, v_ref, sq_ref, sk_ref, sv_ref):
    pairs = range(MLA_HEADS // 2)
    grp = lambda i: slice(i * 2 * LANES, (i + 1) * 2 * LANES)

    def reduce_stage(rows):
        h = _rms(x_ref[rows, :], gmix_ref[...]).astype(BF16)
        rc, ra, rb = rc_ref[rows, :], ra_ref[rows, :], rb_ref[rows, :]

        def rope(t):
            return t * rc + pltpu.roll(t, LANES - 16, 1) * ra + pltpu.roll(t, 16, 1) * rb

        c_q = _rms(_dot(h, win_ref[:, 0:256]), gqa_ref[...]).astype(BF16)
        c_kv = _rms(_dot(h, win_ref[:, 256:384]), gkva_ref[...]).astype(BF16)
        hkr = _dot(h, win_ref[:, 384:512])
        kr_ms = jnp.sum(hkr * hkr, axis=-1, keepdims=True) * (1.0 / MLA_ROPE)
        k_pe = rope(hkr * lax.rsqrt(kr_ms + RMS_EPS) * gkr_ref[...])
        return dict(rows=rows, h=h, c_q=c_q, c_kv=c_kv, k_pe=k_pe, rope=rope)

    def raw_stage(st):
        h, c_q, c_kv, rows = st["h"], st["c_q"], st["c_kv"], st["rows"]
        st["q_raw"] = [_dot(c_q, wuq_ref[:, grp(i)]) for i in pairs]
        st["k_raw"] = [_dot(c_kv, wk_ref[:, grp(i)]) for i in pairs]
        st["s_raw"] = [_dot(h, win_ref[:, 512 + i * 2 * LANES:512 + (i + 1) * 2 * LANES])
                       for i in range(3)]
        sv_ref[rows, :] = _dot(h, win_ref[:, 1280:1536]).astype(BF16)
        for i in pairs:
            v_ref[rows, grp(i)] = (_dot(c_kv, wv_ref[:, grp(i)]) + vones_ref[:, grp(i)]).astype(BF16)

    def stats_stage(st):
        st["q_ms"] = [_segment_mean_sq(t, seg_mla_ref) for t in st["q_raw"]]
        st["k_ms"] = [_segment_mean_sq(t, seg_mla_ref) for t in st["k_raw"]]
        st["s_ms"] = [_segment_mean_sq(t, seg_swa_ref) for t in st["s_raw"]]

    def store_stage(st):
        rows, rope = st["rows"], st["rope"]
        for i in pairs:
            q = st["q_raw"][i] * lax.rsqrt(st["q_ms"][i] + RMS_EPS) * gq_ref[:, grp(i)]
            k = st["k_raw"][i] * lax.rsqrt(st["k_ms"][i] + RMS_EPS) * gk_ref[:, grp(i)]
            for j in range(2):
                blk = slice(j * LANES, (j + 1) * LANES)
                dst = slice((2 * i + j) * LANES, (2 * i + j + 1) * LANES)
                q_ref[rows, dst] = rope(q[:, blk]).astype(BF16)
                k_ref[rows, dst] = (k[:, blk] + st["k_pe"]).astype(BF16)
        s_raw, s_ms = st["s_raw"], st["s_ms"]
        for i in range(2):
            sq_ref[rows, grp(i)] = (s_raw[i] * lax.rsqrt(s_ms[i] + RMS_EPS) * gsq_ref[:, grp(i)]).astype(BF16)
        sk_ref[rows, :] = (s_raw[2] * lax.rsqrt(s_ms[2] + RMS_EPS) * gsk_ref[...]).astype(BF16)

    state = reduce_stage(slice(None))
    raw_stage(state)
    stats_stage(state)
    store_stage(state)


def _proj_call(x1, batch, seq, consts):
    t, d = x1.shape
    assert seq % PROJ_ROWS == 0
    steps = seq // PROJ_ROWS
    row = lambda j, b: (b * steps + j, 0)
    pos = lambda j, b: (j, 0)
    in_specs = [pl.BlockSpec((PROJ_ROWS, d), row)]
    in_specs += [_const_spec(c.shape) for c in consts[:-3]]
    in_specs += [pl.BlockSpec((PROJ_ROWS, LANES), pos)] * 3
    widths = (MLA_HEADS * LANES, MLA_HEADS * LANES, MLA_HEADS * LANES,
              SWA_HEADS * SWA_HEAD_DIM, 2 * LANES, 2 * LANES)
    out_shape = [jax.ShapeDtypeStruct((t, w), BF16) for w in widths]
    out_specs = [pl.BlockSpec((PROJ_ROWS, w), row) for w in widths]
    block_bytes = (_nbytes((PROJ_ROWS, d), F32) + sum(_nbytes(c.shape, c.dtype) for c in consts[:-3])
                   + 3 * _nbytes((PROJ_ROWS, LANES), F32)
                   + sum(_nbytes((PROJ_ROWS, w), BF16) for w in widths))
    return pl.pallas_call(
        _proj_kernel,
        out_shape=out_shape,
        grid=(steps, batch),
        in_specs=in_specs,
        out_specs=out_specs,
        compiler_params=pltpu.CompilerParams(
            dimension_semantics=("parallel", "parallel"),
            vmem_limit_bytes=_vmem_limit(block_bytes)),
        name="proj",
    )(x1, *consts)


def _mla_kernel(q_ref, k_ref, v_ref, o_ref):
    lane = lax.broadcasted_iota(jnp.int32, (q_ref.shape[0], LANES), 1)
    outs = {}

    def scores(h):
        blk = slice(h * LANES, (h + 1) * LANES)
        return _dot_nt(q_ref[:, blk], k_ref[:, blk])

    def finish(h, s):
        m = jnp.max(s, axis=-1, keepdims=True)
        o = None
        for c in range(0, s.shape[1], MLA_KEY_CHUNK):
            p = jnp.exp2(s[:, c:c + MLA_KEY_CHUNK] - m).astype(BF16)
            pv = _dot(p, v_ref[c:c + MLA_KEY_CHUNK, h * LANES:(h + 1) * LANES])
            o = pv if o is None else o + pv
        outs[h] = o / pltpu.roll(o, LANES // 2, 1)
        if h % 2 == 1:
            o_ref[:, (h // 2) * LANES:(h // 2 + 1) * LANES] = (
                jnp.where(lane < LANES // 2, outs[h - 1], outs[h]).astype(BF16))

    s = scores(0)
    for h in range(MLA_HEADS):
        s_next = scores(h + 1) if h + 1 < MLA_HEADS else None
        finish(h, s)
        s = s_next


def _mla_call(q, k, v, batch, seq):
    t, width = q.shape
    assert seq % MLA_ROWS == 0
    steps = seq // MLA_ROWS
    q_spec = pl.BlockSpec((MLA_ROWS, width), lambda b, i: (b * steps + i, 0))
    kv_spec = pl.BlockSpec((seq, width), lambda b, i: (b, 0))
    o_spec = pl.BlockSpec((MLA_ROWS, MLA_HEADS * MLA_V), lambda b, i: (b * steps + i, 0))
    block_bytes = (2 * _nbytes((seq, width), BF16) + 2 * _nbytes((MLA_ROWS, width), BF16)
                   + 4 * _nbytes((MLA_ROWS, seq), F32))
    return pl.pallas_call(
        _mla_kernel,
        out_shape=jax.ShapeDtypeStruct((t, MLA_HEADS * MLA_V), BF16),
        grid=(batch, steps),
        in_specs=[q_spec, kv_spec, kv_spec],
        out_specs=o_spec,
        compiler_params=pltpu.CompilerParams(
            dimension_semantics=("parallel", "parallel"),
            vmem_limit_bytes=_vmem_limit(block_bytes)),
        name="mla_attn",
    )(q, k, v)


def _swa_key_start(i, seq):
    return jnp.clip(i * SWA_Q_ROWS - WINDOW, 0, seq - SWA_KEYS)


def _swa_bias_tables(seq):
    steps = seq // SWA_Q_ROWS
    r = jnp.arange(SWA_Q_ROWS)[:, None]
    c = jnp.arange(SWA_KEYS)[None, :]
    slopes = 2.0 ** (-(8.0 / SWA_HEADS) * jnp.arange(1, SWA_HEADS + 1, dtype=F32))
    cases = []
    for i in (0, 1, steps - 1):
        dist = jnp.abs(c - r + (_swa_key_start(i, seq) - i * SWA_Q_ROWS)).astype(F32)
        bias = -LOG2E * slopes[:, None, None] * dist[None]
        cases.append(jnp.where((dist <= WINDOW)[None], bias, -jnp.inf))
    return jnp.stack(cases)


def _swa_stages(sink_ref, bias_ref, q_ref, k_ref, v_ref, o_ref):
    seq = k_ref.shape[0]
    blocks = seq // SWA_Q_ROWS
    rows = SWA_Q_ROWS
    blocks_per_step = q_ref.shape[0] // rows
    lane = lax.broadcasted_iota(jnp.int32, (rows, LANES), 1)
    low_half = lane < LANES // 2
    kv_lane = lax.broadcasted_iota(jnp.int32, (SWA_KEYS, 2 * LANES), 1)
    low_kv = (kv_lane & (LANES // 2)) == 0

    pairs = range(SWA_GROUP // 2)
    q_block = lambda g, pr: slice((g * (SWA_GROUP // 2) + pr) * LANES,
                                  (g * (SWA_GROUP // 2) + pr + 1) * LANES)

    def score_units(u):
        blk = pl.program_id(1) * blocks_per_step + u
        key0 = pl.multiple_of(_swa_key_start(blk, seq), LANES)
        case = jnp.where(blk == 0, 0, jnp.where(blk == blocks - 1, 2, 1))
        qrows = slice(u * rows, (u + 1) * rows)
        kwin = k_ref[pl.ds(key0, SWA_KEYS), :].astype(F32)
        vwin = v_ref[pl.ds(key0, SWA_KEYS), :].astype(F32)
        k_lo = jnp.where(low_kv, kwin, 0.0).astype(BF16)
        k_hi = jnp.where(low_kv, 0.0, kwin).astype(BF16)
        v_lo = jnp.where(low_kv, vwin, 0.0).astype(BF16)
        v_hi = jnp.where(low_kv, 0.0, vwin).astype(BF16)
        units = []
        for g in range(SWA_KV_HEADS):
            grp = slice(g * LANES, (g + 1) * LANES)
            k_bd = jnp.concatenate([k_lo[:, grp], k_hi[:, grp]], axis=0)
            v_bd = jnp.concatenate([v_lo[:, grp], v_hi[:, grp]], axis=0)
            q2 = jnp.concatenate([q_ref[qrows, q_block(g, pr)] for pr in pairs], axis=0)
            units.append((g, qrows, case, _dot_nt(q2, k_bd), v_bd))
        return units

    def finish(unit):
        g, qrows, case, scores, v_bd = unit
        p_rows, denoms = [], []
        for pr in pairs:
            p_halves, d_halves = [], []
            for half in range(2):
                head = g * SWA_GROUP + 2 * pr + half
                logits = (scores[pr * rows:(pr + 1) * rows, half * SWA_KEYS:(half + 1) * SWA_KEYS]
                          + bias_ref[case, head])
                sink = sink_ref[head] * LOG2E
                m = jnp.maximum(jnp.max(logits, axis=-1, keepdims=True), sink)
                p = jnp.exp2(logits - m)
                d_halves.append(jnp.sum(p, axis=-1, keepdims=True) + jnp.exp2(sink - m))
                p_halves.append(p.astype(BF16))
            p_rows.append(jnp.concatenate(p_halves, axis=1))
            denoms.append(jnp.where(low_half, d_halves[0], d_halves[1]))
        o = _dot(jnp.concatenate(p_rows, axis=0), v_bd)
        for pr in pairs:
            o_ref[qrows, q_block(g, pr)] = (o[pr * rows:(pr + 1) * rows] / denoms[pr]).astype(BF16)

    return score_units, finish


def _swa_kernel(sink_ref, bias_ref, q_ref, k_ref, v_ref, o_ref):
    score_units, finish = _swa_stages(sink_ref, bias_ref, q_ref, k_ref, v_ref, o_ref)
    blocks = q_ref.shape[0] // SWA_Q_ROWS
    pending = score_units(0)
    for u in range(blocks):
        current = pending
        if u + 1 < blocks:
            pending = score_units(u + 1)
        for unit in current:
            finish(unit)


def _swa_call(sink, sq, sk, sv, batch, seq):
    t, width = sq.shape
    assert seq % SWA_ROWS == 0 and SWA_ROWS % SWA_Q_ROWS == 0
    assert seq >= SWA_KEYS and seq // SWA_Q_ROWS >= 3
    steps = seq // SWA_ROWS
    q_spec = pl.BlockSpec((SWA_ROWS, width), lambda b, i: (b * steps + i, 0))
    kv_spec = pl.BlockSpec((seq, sk.shape[1]), lambda b, i: (b, 0))
    bias = _swa_bias_tables(seq)
    block_bytes = (2 * _nbytes((seq, sk.shape[1]), BF16) + 2 * _nbytes((SWA_ROWS, width), BF16)
                   + _nbytes(bias.shape, F32) + 4 * _nbytes((SWA_ROWS, 2 * SWA_KEYS), F32))
    return pl.pallas_call(
        _swa_kernel,
        out_shape=jax.ShapeDtypeStruct((t, width), BF16),
        grid=(batch, steps),
        in_specs=[pl.BlockSpec(memory_space=pltpu.SMEM), _const_spec(bias.shape),
                  q_spec, kv_spec, kv_spec],
        out_specs=q_spec,
        compiler_params=pltpu.CompilerParams(
            dimension_semantics=("parallel", "parallel"),
            vmem_limit_bytes=_vmem_limit(block_bytes)),
        name="swa_attn",
    )(sink, bias, sq, sk, sv)


def _head_blocks(w, heads, width, lo, hi, at=0):
    rows = w.shape[0]
    part = w.reshape(rows, heads, width)[:, :, lo:hi]
    part = jnp.pad(part, ((0, 0), (0, 0), (at, LANES - at - (hi - lo))))
    return part.reshape(rows, heads * LANES)


def _proj_constants(seq, g_mix, w_in, g_q_a, w_uq, g_kv_a, w_ukv,
                    g_qn, g_qr, g_kn, g_kr, g_swa_q, g_swa_k):
    d = w_in.shape[0]
    q_rank, kv_rank = g_q_a.shape[0], g_kv_a.shape[0]
    o_kr = q_rank + kv_rank
    o_sq = o_kr + MLA_ROPE
    o_sk = o_sq + SWA_HEADS * SWA_HEAD_DIM
    o_sv = o_sk + SWA_KV_HEADS * SWA_HEAD_DIM
    kr_block = jnp.pad(w_in[:, o_kr:o_sq], ((0, 0), (MLA_NOPE, LANES - MLA_NOPE - MLA_ROPE)))
    dup = lambda w: jnp.repeat(w.reshape(d, SWA_KV_HEADS, 1, SWA_HEAD_DIM), 2, axis=2).reshape(d, -1)
    win = jnp.concatenate(
        [w_in[:, :o_kr], kr_block, w_in[:, o_sq:o_sk], dup(w_in[:, o_sk:o_sv]), dup(w_in[:, o_sv:])],
        axis=1).astype(BF16)

    qk_width = MLA_NOPE + MLA_ROPE
    wuq = _head_blocks(w_uq, MLA_HEADS, qk_width, 0, qk_width).astype(BF16)
    kv_width = MLA_NOPE + MLA_V
    wk = _head_blocks(w_ukv, MLA_HEADS, kv_width, 0, MLA_NOPE).astype(BF16)
    wv_even = _head_blocks(w_ukv, MLA_HEADS, kv_width, MLA_NOPE, kv_width, at=0)
    wv_odd = _head_blocks(w_ukv, MLA_HEADS, kv_width, MLA_NOPE, kv_width, at=LANES // 2)
    head_of_col = jnp.arange(MLA_HEADS * LANES) // LANES
    odd_col = (head_of_col % 2 == 1)[None, :]
    wv = jnp.where(odd_col, wv_odd, wv_even).astype(BF16)
    lane = jnp.arange(MLA_HEADS * LANES) % LANES
    vones = ((lane >= LANES // 2) != odd_col[0]).astype(F32)[None, :]

    scale = LOG2E / math.sqrt(qk_width)
    zeros = lambda n: jnp.zeros((n,), F32)
    gq = jnp.tile(jnp.concatenate([g_qn, g_qr, zeros(LANES - qk_width)]) * scale, MLA_HEADS)[None, :]
    gk = jnp.tile(jnp.concatenate([g_kn, zeros(LANES - MLA_NOPE)]), MLA_HEADS)[None, :]
    gkr = jnp.concatenate([zeros(MLA_NOPE), g_kr, zeros(LANES - qk_width)])[None, :]
    gsq = jnp.tile(g_swa_q * (LOG2E / math.sqrt(SWA_HEAD_DIM)), SWA_HEADS)[None, :]
    gsk = jnp.tile(g_swa_k, 2 * SWA_KV_HEADS)[None, :]

    idx = jnp.arange(2 * LANES)
    in_blk = idx % LANES
    seg_id = jnp.where(in_blk < MLA_NOPE, 0, jnp.where(in_blk < qk_width, 1, 2)) + 3 * (idx // LANES)
    seg_len = jnp.where(in_blk < MLA_NOPE, MLA_NOPE, MLA_ROPE).astype(F32)
    same = (seg_id[:, None] == seg_id[None, :]) & (in_blk < qk_width)[:, None]
    seg_mla = jnp.where(same, 1.0 / seg_len[None, :], 0.0).astype(BF16)
    seg_swa = jnp.where((idx // SWA_HEAD_DIM)[:, None] == (idx // SWA_HEAD_DIM)[None, :],
                        1.0 / SWA_HEAD_DIM, 0.0).astype(BF16)

    pos = np.arange(seq, dtype=np.float64)
    inv = 1.0 / (ROPE_THETA ** (np.arange(0, MLA_ROPE, 2, dtype=np.float64) / MLA_ROPE))
    ang = pos[:, None] * inv[None, :]
    cos, sin = np.cos(ang), np.sin(ang)
    half = MLA_ROPE // 2
    pad_lo = np.ones((seq, MLA_NOPE))
    pad_hi = np.ones((seq, LANES - qk_width))
    zero_half = np.zeros((seq, half))
    rc = np.concatenate([pad_lo, cos, cos, pad_hi], axis=1).astype(np.float32)
    ra = np.concatenate([0 * pad_lo, -sin, zero_half, 0 * pad_hi], axis=1).astype(np.float32)
    rb = np.concatenate([0 * pad_lo, zero_half, sin, 0 * pad_hi], axis=1).astype(np.float32)

    row = lambda g: g.astype(F32)[None, :]
    return (row(g_mix), win, row(g_q_a), wuq, row(g_kv_a), wk, wv, vones,
            gq, gk, gkr, gsq, gsk, seg_mla, seg_swa, rc, ra, rb)


def kernel(x, g_ffn1, w1_gate, w1_up, w1_down, g_mix, w_in, g_q_a, w_uq, g_kv_a, w_ukv,
           g_mla_qn, g_mla_qr, g_mla_kn, g_mla_kr, g_swa_q, g_swa_k, sink, w_o,
           g_ffn2, w2_gate, w2_up, w2_down):
    batch, seq, d = x.shape
    x2d = x.reshape(batch * seq, d)
    bf = lambda w: w.astype(BF16)
    row = lambda g: g.astype(F32)[None, :]

    x1 = _ffn_call(x2d, row(g_ffn1), w1_gate, w1_up, w1_down)
    consts = _proj_constants(seq, g_mix, w_in, g_q_a, w_uq, g_kv_a, w_ukv,
                             g_mla_qn, g_mla_qr, g_mla_kn, g_mla_kr, g_swa_q, g_swa_k)
    q, k, v, sq, sk, sv = _proj_call(x1, batch, seq, consts)
    y_a = _mla_call(q, k, v, batch, seq)
    y_b = _swa_call(sink.astype(F32), sq, sk, sv, batch, seq)
    out = _ffn_call(x1, row(g_ffn2), w2_gate, w2_up, w2_down,
                    mix=(y_a, y_b, bf(w_o)))
    return out.reshape(batch, seq, d)
```

```python
import math

import jax
import jax.numpy as jnp
from jax import lax
from jax.experimental import pallas as pl
from jax.experimental.pallas import tpu as pltpu

F32 = jnp.float32
BF16 = jnp.bfloat16

RMS_EPS = 1e-6
MLA_HEADS = 8
MLA_NOPE = 64
MLA_ROPE = 32
MLA_V = 64
ROPE_THETA = 10000.0
SWA_HEADS = 8
SWA_KV_HEADS = 2
SWA_HEAD_DIM = 64
SWA_GROUP = SWA_HEADS // SWA_KV_HEADS
WINDOW = 128
LOG2E = math.log2(math.e)

LANES = 128
V7X_VMEM_BYTES = 64 * 1024 * 1024
VMEM_HEADROOM_BYTES = 8 * 1024 * 1024

FFN_ROWS = 512
FFN_CHUNK = 256
PROJ_ROWS = 1024
MLA_ROWS = 1024
MLA_KEY_CHUNK = 512
SWA_ROWS = 1024
SWA_Q_ROWS = 128
SWA_KEYS = SWA_Q_ROWS + 2 * WINDOW


def _vmem_limit(block_bytes):
    want = 2 * block_bytes + 16 * 1024 * 1024
    return int(min(want, V7X_VMEM_BYTES - VMEM_HEADROOM_BYTES))


def _nbytes(shape, dtype):
    return math.prod(shape) * jnp.dtype(dtype).itemsize


def _const_spec(shape):
    zeros = (0,) * len(shape)
    return pl.BlockSpec(shape, lambda *_: zeros, pipeline_mode=pl.Buffered(1))


def _rms(x, gain):
    ms = jnp.mean(x * x, axis=-1, keepdims=True)
    return x * lax.rsqrt(ms + RMS_EPS) * gain


def _dot(a, b):
    return jnp.dot(a, b, preferred_element_type=F32)


def _dot_nt(a, b):
    return lax.dot_general(a, b, (((1,), (1,)), ((), ())), preferred_element_type=F32)


def _swiglu_residual(x, gain_ref, wg_ref, wu_ref, wd_ref, act_ref):
    h = _rms(x, gain_ref[...])
    d_ff = wg_ref.shape[1]
    for c in range(d_ff // FFN_CHUNK):
        cols = slice(c * FFN_CHUNK, (c + 1) * FFN_CHUNK)
        gate = _dot(h, wg_ref[:, cols])
        up = _dot(h, wu_ref[:, cols])
        act_ref[:, cols] = gate / (1.0 + jnp.exp(-gate)) * up
    return x + 0.5 * _dot(act_ref[...], wd_ref[...])


def _ffn_kernel(x_ref, gain_ref, wg_ref, wu_ref, wd_ref, o_ref, act_ref):
    o_ref[...] = _swiglu_residual(x_ref[...], gain_ref, wg_ref, wu_ref, wd_ref, act_ref)


def _mix_ffn_kernel(x_ref, ya_ref, yb_ref, wo_ref, gain_ref, wg_ref, wu_ref, wd_ref,
                    o_ref, act_ref):
    half = ya_ref.shape[1]
    x2 = (x_ref[...] + _dot(ya_ref[...], wo_ref[:half, :])
          + _dot(yb_ref[...], wo_ref[half:, :]))
    o_ref[...] = _swiglu_residual(x2, gain_ref, wg_ref, wu_ref, wd_ref, act_ref)


def _ffn_call(x, gain, wg, wu, wd, mix=None):
    t, d = x.shape
    d_ff = wg.shape[1]
    assert t % FFN_ROWS == 0 and d_ff % FFN_CHUNK == 0
    row = lambda i: (i, 0)
    x_spec = pl.BlockSpec((FFN_ROWS, d), row)
    w_specs = [_const_spec((1, d)), _const_spec((d, d_ff)), _const_spec((d, d_ff)),
               _const_spec((d_ff, d))]
    block_bytes = (2 * _nbytes((FFN_ROWS, d), F32) + 3 * _nbytes((d, d_ff), wg.dtype)
                   + _nbytes((FFN_ROWS, d_ff), F32))
    if mix is None:
        body, operands, in_specs = _ffn_kernel, (x, gain, wg, wu, wd), [x_spec] + w_specs
    else:
        ya, yb, wo = mix
        half = ya.shape[1]
        y_spec = pl.BlockSpec((FFN_ROWS, half), row)
        body = _mix_ffn_kernel
        operands = (x, ya, yb, wo, gain, wg, wu, wd)
        in_specs = [x_spec, y_spec, y_spec, _const_spec(wo.shape)] + w_specs
        block_bytes += 2 * _nbytes((FFN_ROWS, half), BF16) + _nbytes(wo.shape, BF16)
    return pl.pallas_call(
        body,
        out_shape=jax.ShapeDtypeStruct((t, d), F32),
        grid=(t // FFN_ROWS,),
        in_specs=in_specs,
        out_specs=x_spec,
        scratch_shapes=[pltpu.VMEM((FFN_ROWS, d_ff), F32)],
        compiler_params=pltpu.CompilerParams(
            dimension_semantics=("parallel",), vmem_limit_bytes=_vmem_limit(block_bytes)),
        name="ffn" if mix is None else "mix_ffn",
    )(*operands)


def _segment_mean_sq(x, seg_ref):
    return _dot((x * x).astype(BF16), seg_ref[...])


def _proj_kernel(x_ref, gmix_ref, win_ref, gqa_ref, wuq_ref, gkva_ref, wk_ref, wv_ref,
                 vones_ref, gq_ref, gk_ref, gkr_ref, gsq_ref, gsk_ref,
                 seg_mla_ref, seg_swa_ref, rc_ref, ra_ref, rb_ref,
                 q_ref, k_ref, v_ref, sq_ref, sk_ref, sv_ref):
    pairs = range(MLA_HEADS // 2)
    grp = lambda i: slice(i * 2 * LANES, (i + 1) * 2 * LANES)

    def reduce_stage(rows):
        h = _rms(x_ref[rows, :], gmix_ref[...]).astype(BF16)
        rc, ra, rb = rc_ref[rows, :], ra_ref[rows, :], rb_ref[rows, :]

        def rope(t):
            return t * rc + pltpu.roll(t, LANES - 16, 1) * ra + pltpu.roll(t, 16, 1) * rb

        c_q = _rms(_dot(h, win_ref[:, 0:256]), gqa_ref[...]).astype(BF16)
        c_kv = _rms(_dot(h, win_ref[:, 256:384]), gkva_ref[...]).astype(BF16)
        hkr = _dot(h, win_ref[:, 384:512])
        kr_ms = jnp.sum(hkr * hkr, axis=-1, keepdims=True) * (1.0 / MLA_ROPE)
        k_pe = rope(hkr * lax.rsqrt(kr_ms + RMS_EPS) * gkr_ref[...])
        return dict(rows=rows, h=h, c_q=c_q, c_kv=c_kv, k_pe=k_pe, rope=rope)

    def raw_stage(st):
        h, c_q, c_kv, rows = st["h"], st["c_q"], st["c_kv"], st["rows"]
        st["q_raw"] = [_dot(c_q, wuq_ref[:, grp(i)]) for i in pairs]
        st["k_raw"] = [_dot(c_kv, wk_ref[:, grp(i)]) for i in pairs]
        st["s_raw"] = [_dot(h, win_ref[:, 512 + i * 2 * LANES:512 + (i + 1) * 2 * LANES])
                       for i in range(3)]
        sv_ref[rows, :] = _dot(h, win_ref[:, 1280:1536]).astype(BF16)
        for i in pairs:
            v_ref[rows, grp(i)] = (_dot(c_kv, wv_ref[:, grp(i)]) + vones_ref[:, grp(i)]).astype(BF16)

    def stats_stage(st):
        st["q_ms"] = [_segment_mean_sq(t, seg_mla_ref) for t in st["q_raw"]]
        st["k_ms"] = [_segment_mean_sq(t, seg_mla_ref) for t in st["k_raw"]]
        st["s_ms"] = [_segment_mean_sq(t, seg_swa_ref) for t in st["s_raw"]]

    def store_stage(st):
        rows, rope = st["rows"], st["rope"]
        for i in pairs:
            q = st["q_raw"][i] * lax.rsqrt(st["q_ms"][i] + RMS_EPS) * gq_ref[:, grp(i)]
            k = st["k_raw"][i] * lax.rsqrt(st["k_ms"][i] + RMS_EPS) * gk_ref[:, grp(i)]
            for j in range(2):
                blk = slice(j * LANES, (j + 1) * LANES)
                dst = slice((2 * i + j) * LANES, (2 * i + j + 1) * LANES)
                q_ref[rows, dst] = rope(q[:, blk]).astype(BF16)
                k_ref[rows, dst] = (k[:, blk] + st["k_pe"]).astype(BF16)
        s_raw, s_ms = st["s_raw"], st["s_ms"]
        for i in range(2):
            sq_ref[rows, grp(i)] = (s_raw[i] * lax.rsqrt(s_ms[i] + RMS_EPS) * gsq_ref[:, grp(i)]).astype(BF16)
        sk_ref[rows, :] = (s_raw[2] * lax.rsqrt(s_ms[2] + RMS_EPS) * gsk_ref[...]).astype(BF16)

    state = reduce_stage(slice(None))
    raw_stage(state)
    stats_stage(state)
    store_stage(state)


def _proj_call(x1, batch, seq, consts):
    t, d = x1.shape
    assert seq % PROJ_ROWS == 0
    steps = seq // PROJ_ROWS
    row = lambda j, b: (b * steps + j, 0)
    pos = lambda j, b: (j, 0)
    in_specs = [pl.BlockSpec((PROJ_ROWS, d), row)]
    in_specs += [_const_spec(c.shape) for c in consts[:-3]]
    in_specs += [pl.BlockSpec((PROJ_ROWS, LANES), pos)] * 3
    widths = (MLA_HEADS * LANES, MLA_HEADS * LANES, MLA_HEADS * LANES,
              SWA_HEADS * SWA_HEAD_DIM, 2 * LANES, 2 * LANES)
    out_shape = [jax.ShapeDtypeStruct((t, w), BF16) for w in widths]
    out_specs = [pl.BlockSpec((PROJ_ROWS, w), row) for w in widths]
    block_bytes = (_nbytes((PROJ_ROWS, d), F32) + sum(_nbytes(c.shape, c.dtype) for c in consts[:-3])
                   + 3 * _nbytes((PROJ_ROWS, LANES), F32)
                   + sum(_nbytes((PROJ_ROWS, w), BF16) for w in widths))
    return pl.pallas_call(
        _proj_kernel,
        out_shape=out_shape,
        grid=(steps, batch),
        in_specs=in_specs,
        out_specs=out_specs,
        compiler_params=pltpu.CompilerParams(
            dimension_semantics=("parallel", "parallel"),
            vmem_limit_bytes=_vmem_limit(block_bytes)),
        name="proj",
    )(x1, *consts)


def _mla_kernel(q_ref, k_ref, v_ref, o_ref):
    lane = lax.broadcasted_iota(jnp.int32, (q_ref.shape[0], LANES), 1)
    outs = {}

    def scores(h):
        blk = slice(h * LANES, (h + 1) * LANES)
        return _dot_nt(q_ref[:, blk], k_ref[:, blk])

    def finish(h, s):
        m = jnp.max(s, axis=-1, keepdims=True)
        o = None
        for c in range(0, s.shape[1], MLA_KEY_CHUNK):
            p = jnp.exp2(s[:, c:c + MLA_KEY_CHUNK] - m).astype(BF16)
            pv = _dot(p, v_ref[c:c + MLA_KEY_CHUNK, h * LANES:(h + 1) * LANES])
            o = pv if o is None else o + pv
        outs[h] = o / pltpu.roll(o, LANES // 2, 1)
        if h % 2 == 1:
            o_ref[:, (h // 2) * LANES:(h // 2 + 1) * LANES] = (
                jnp.where(lane < LANES // 2, outs[h - 1], outs[h]).astype(BF16))

    s = scores(0)
    for h in range(MLA_HEADS):
        s_next = scores(h + 1) if h + 1 < MLA_HEADS else None
        finish(h, s)
        s = s_next


def _mla_call(q, k, v, batch, seq):
    t, width = q.shape
    assert seq % MLA_ROWS == 0
    steps = seq // MLA_ROWS
    q_spec = pl.BlockSpec((MLA_ROWS, width), lambda b, i: (b * steps + i, 0))
    kv_spec = pl.BlockSpec((seq, width), lambda b, i: (b, 0))
    o_spec = pl.BlockSpec((MLA_ROWS, MLA_HEADS * MLA_V), lambda b, i: (b * steps + i, 0))
    block_bytes = (2 * _nbytes((seq, width), BF16) + 2 * _nbytes((MLA_ROWS, width), BF16)
                   + 4 * _nbytes((MLA_ROWS, seq), F32))
    return pl.pallas_call(
        _mla_kernel,
        out_shape=jax.ShapeDtypeStruct((t, MLA_HEADS * MLA_V), BF16),
        grid=(batch, steps),
        in_specs=[q_spec, kv_spec, kv_spec],
        out_specs=o_spec,
        compiler_params=pltpu.CompilerParams(
            dimension_semantics=("parallel", "parallel"),
            vmem_limit_bytes=_vmem_limit(block_bytes)),
        name="mla_attn",
    )(q, k, v)


def _swa_key_start(i, seq):
    return jnp.clip(i * SWA_Q_ROWS - WINDOW, 0, seq - SWA_KEYS)


def _swa_bias_tables(seq):
    steps = seq // SWA_Q_ROWS
    r = jnp.arange(SWA_Q_ROWS)[:, None]
    c = jnp.arange(SWA_KEYS)[None, :]
    slopes = 2.0 ** (-(8.0 / SWA_HEADS) * jnp.arange(1, SWA_HEADS + 1, dtype=F32))
    cases = []
    for i in (0, 1, steps - 1):
        dist = jnp.abs(c - r + (_swa_key_start(i, seq) - i * SWA_Q_ROWS)).astype(F32)
        bias = -LOG2E * slopes[:, None, None] * dist[None]
        cases.append(jnp.where((dist <= WINDOW)[None], bias, -jnp.inf))
    return jnp.stack(cases)


def _swa_stages(sink_ref, bias_ref, q_ref, k_ref, v_ref, o_ref):
    seq = k_ref.shape[0]
    blocks = seq // SWA_Q_ROWS
    rows = SWA_Q_ROWS
    blocks_per_step = q_ref.shape[0] // rows
    lane = lax.broadcasted_iota(jnp.int32, (rows, LANES), 1)
    low_half = lane < LANES // 2
    kv_lane = lax.broadcasted_iota(jnp.int32, (SWA_KEYS, 2 * LANES), 1)
    low_kv = (kv_lane & (LANES // 2)) == 0

    pairs = range(SWA_GROUP // 2)
    q_block = lambda g, pr: slice((g * (SWA_GROUP // 2) + pr) * LANES,
                                  (g * (SWA_GROUP // 2) + pr + 1) * LANES)

    def score_units(u):
        blk = pl.program_id(1) * blocks_per_step + u
        key0 = pl.multiple_of(_swa_key_start(blk, seq), LANES)
        case = jnp.where(blk == 0, 0, jnp.where(blk == blocks - 1, 2, 1))
        qrows = slice(u * rows, (u + 1) * rows)
        kwin = k_ref[pl.ds(key0, SWA_KEYS), :].astype(F32)
        vwin = v_ref[pl.ds(key0, SWA_KEYS), :].astype(F32)
        k_lo = jnp.where(low_kv, kwin, 0.0).astype(BF16)
        k_hi = jnp.where(low_kv, 0.0, kwin).astype(BF16)
        v_lo = jnp.where(low_kv, vwin, 0.0).astype(BF16)
        v_hi = jnp.where(low_kv, 0.0, vwin).astype(BF16)
        units = []
        for g in range(SWA_KV_HEADS):
            grp = slice(g * LANES, (g + 1) * LANES)
            k_bd = jnp.concatenate([k_lo[:, grp], k_hi[:, grp]], axis=0)
            v_bd = jnp.concatenate([v_lo[:, grp], v_hi[:, grp]], axis=0)
            q2 = jnp.concatenate([q_ref[qrows, q_block(g, pr)] for pr in pairs], axis=0)
            units.append((g, qrows, case, _dot_nt(q2, k_bd), v_bd))
        return units

    def finish(unit):
        g, qrows, case, scores, v_bd = unit
        p_rows, denoms = [], []
        for pr in pairs:
            p_halves, d_halves = [], []
            for half in range(2):
                head = g * SWA_GROUP + 2 * pr + half
                logits = (scores[pr * rows:(pr + 1) * rows, half * SWA_KEYS:(half + 1) * SWA_KEYS]
                          + bias_ref[case, head])
                sink = sink_ref[head] * LOG2E
                m = jnp.maximum(jnp.max(logits, axis=-1, keepdims=True), sink)
                p = jnp.exp2(logits - m)
                d_halves.append(jnp.sum(p, axis=-1, keepdims=True) + jnp.exp2(sink - m))
                p_halves.append(p.astype(BF16))
            p_rows.append(jnp.concatenate(p_halves, axis=1))
            denoms.append(jnp.where(low_half, d_halves[0], d_halves[1]))
        o = _dot(jnp.concatenate(p_rows, axis=0), v_bd)
        for pr in pairs:
            o_ref[qrows, q_block(g, pr)] = (o[pr * rows:(pr + 1) * rows] / denoms[pr]).astype(BF16)

    return score_units, finish


def _swa_kernel(sink_ref, bias_ref, q_ref, k_ref, v_ref, o_ref):
    score_units, finish = _swa_stages(sink_ref, bias_ref, q_ref, k_ref, v_ref, o_ref)
    blocks = q_ref.shape[0] // SWA_Q_ROWS
    pending = score_units(0)
    for u in range(blocks):
        current = pending
        if u + 1 < blocks:
            pending = score_units(u + 1)
        for unit in current:
            finish(unit)


def _swa_call(sink, sq, sk, sv, batch, seq):
    t, width = sq.shape
    assert seq % SWA_ROWS == 0 and SWA_ROWS % SWA_Q_ROWS == 0
    assert seq >= SWA_KEYS and seq // SWA_Q_ROWS >= 3
    steps = seq // SWA_ROWS
    q_spec = pl.BlockSpec((SWA_ROWS, width), lambda b, i: (b * steps + i, 0))
    kv_spec = pl.BlockSpec((seq, 2 * LANES), lambda b, i: (b, 0))
    bias = _swa_bias_tables(seq)
    block_bytes = (2 * _nbytes((seq, 2 * LANES), BF16) + 2 * _nbytes((SWA_ROWS, width), BF16)
                   + _nbytes(bias.shape, F32) + 4 * _nbytes((SWA_ROWS, 2 * SWA_KEYS), F32))
    return pl.pallas_call(
        _swa_kernel,
        out_shape=jax.ShapeDtypeStruct((t, width), BF16),
        grid=(batch, steps),
        in_specs=[pl.BlockSpec(memory_space=pltpu.SMEM), _const_spec(bias.shape),
                  q_spec, kv_spec, kv_spec],
        out_specs=q_spec,
        compiler_params=pltpu.CompilerParams(
            dimension_semantics=("parallel", "parallel"),
            vmem_limit_bytes=_vmem_limit(block_bytes)),
        name="swa_attn",
    )(sink, bias, sq, sk, sv)


def _head_blocks(w, heads, width, lo, hi, at=0):
    rows = w.shape[0]
    part = w.reshape(rows, heads, width)[:, :, lo:hi]
    part = jnp.pad(part, ((0, 0), (0, 0), (at, LANES - at - (hi - lo))))
    return part.reshape(rows, heads * LANES)


def _proj_constants(seq, g_mix, w_in, g_q_a, w_uq, g_kv_a, w_ukv,
                    g_qn, g_qr, g_kn, g_kr, g_swa_q, g_swa_k):
    d = w_in.shape[0]
    q_rank, kv_rank = g_q_a.shape[0], g_kv_a.shape[0]
    o_kr = q_rank + kv_rank
    o_sq = o_kr + MLA_ROPE
    o_sk = o_sq + SWA_HEADS * SWA_HEAD_DIM
    o_sv = o_sk + SWA_KV_HEADS * SWA_HEAD_DIM
    kr_block = jnp.pad(w_in[:, o_kr:o_sq], ((0, 0), (MLA_NOPE, LANES - MLA_NOPE - MLA_ROPE)))
    dup = lambda w: jnp.repeat(w.reshape(d, SWA_KV_HEADS, 1, SWA_HEAD_DIM), 2, axis=2).reshape(d, -1)
    win = jnp.concatenate(
        [w_in[:, :o_kr], kr_block, w_in[:, o_sq:o_sk], dup(w_in[:, o_sk:o_sv]), dup(w_in[:, o_sv:])],
        axis=1).astype(BF16)

    qk_width = MLA_NOPE + MLA_ROPE
    wuq = _head_blocks(w_uq, MLA_HEADS, qk_width, 0, qk_width).astype(BF16)
    kv_width = MLA_NOPE + MLA_V
    wk = _head_blocks(w_ukv, MLA_HEADS, kv_width, 0, MLA_NOPE).astype(BF16)
    wv_even = _head_blocks(w_ukv, MLA_HEADS, kv_width, MLA_NOPE, kv_width, at=0)
    wv_odd = _head_blocks(w_ukv, MLA_HEADS, kv_width, MLA_NOPE, kv_width, at=LANES // 2)
    head_of_col = jnp.arange(MLA_HEADS * LANES) // LANES
    odd_col = (head_of_col % 2 == 1)[None, :]
    wv = jnp.where(odd_col, wv_odd, wv_even).astype(BF16)
    lane = jnp.arange(MLA_HEADS * LANES) % LANES
    vones = ((lane >= LANES // 2) != odd_col[0]).astype(F32)[None, :]

    scale = LOG2E / math.sqrt(qk_width)
    zeros = lambda n: jnp.zeros((n,), F32)
    gq = jnp.tile(jnp.concatenate([g_qn, g_qr, zeros(LANES - qk_width)]) * scale, MLA_HEADS)[None, :]
    gk = jnp.tile(jnp.concatenate([g_kn, zeros(LANES - MLA_NOPE)]), MLA_HEADS)[None, :]
    gkr = jnp.concatenate([zeros(MLA_NOPE), g_kr, zeros(LANES - qk_width)])[None, :]
    gsq = jnp.tile(g_swa_q * (LOG2E / math.sqrt(SWA_HEAD_DIM)), SWA_HEADS)[None, :]
    gsk = jnp.tile(g_swa_k, 2 * SWA_KV_HEADS)[None, :]

    idx = jnp.arange(2 * LANES)
    in_blk = idx % LANES
    seg_id = jnp.where(in_blk < MLA_NOPE, 0, jnp.where(in_blk < qk_width, 1, 2)) + 3 * (idx // LANES)
    seg_len = jnp.where(in_blk < MLA_NOPE, MLA_NOPE, MLA_ROPE).astype(F32)
    same = (seg_id[:, None] == seg_id[None, :]) & (in_blk < qk_width)[:, None]
    seg_mla = jnp.where(same, 1.0 / seg_len[None, :], 0.0).astype(BF16)
    seg_swa = jnp.where((idx // SWA_HEAD_DIM)[:, None] == (idx // SWA_HEAD_DIM)[None, :],
                        1.0 / SWA_HEAD_DIM, 0.0).astype(BF16)

    pos = jnp.arange(seq, dtype=F32)
    inv = 1.0 / (ROPE_THETA ** (jnp.arange(0, MLA_ROPE, 2, dtype=F32) / MLA_ROPE))
    ang = pos[:, None] * inv[None, :]
    cos, sin = jnp.cos(ang), jnp.sin(ang)
    half = MLA_ROPE // 2
    pad_lo = jnp.ones((seq, MLA_NOPE), F32)
    pad_hi = jnp.ones((seq, LANES - qk_width), F32)
    rc = jnp.concatenate([pad_lo, cos, cos, pad_hi], axis=1)
    ra = jnp.concatenate([0 * pad_lo, -sin, jnp.zeros((seq, half), F32), 0 * pad_hi], axis=1)
    rb = jnp.concatenate([0 * pad_lo, jnp.zeros((seq, half), F32), sin, 0 * pad_hi], axis=1)

    row = lambda g: g.astype(F32)[None, :]
    return (row(g_mix), win, row(g_q_a), wuq, row(g_kv_a), wk, wv, vones,
            gq, gk, gkr, gsq, gsk, seg_mla, seg_swa, rc, ra, rb)


def kernel(x, g_ffn1, w1_gate, w1_up, w1_down, g_mix, w_in, g_q_a, w_uq, g_kv_a, w_ukv,
           g_mla_qn, g_mla_qr, g_mla_kn, g_mla_kr, g_swa_q, g_swa_k, sink, w_o,
           g_ffn2, w2_gate, w2_up, w2_down):
    batch, seq, d = x.shape
    x2d = x.reshape(batch * seq, d)
    bf = lambda w: w.astype(BF16)
    row = lambda g: g.astype(F32)[None, :]

    x1 = _ffn_call(x2d, row(g_ffn1), w1_gate, w1_up, w1_down)
    consts = _proj_constants(seq, g_mix, w_in, g_q_a, w_uq, g_kv_a, w_ukv,
                             g_mla_qn, g_mla_qr, g_mla_kn, g_mla_kr, g_swa_q, g_swa_k)
    q, k, v, sq, sk, sv = _proj_call(x1, batch, seq, consts)
    y_a = _mla_call(q, k, v, batch, seq)
    y_b = _swa_call(sink.astype(F32), sq, sk, sv, batch, seq)
    out = _ffn_call(x1, row(g_ffn2), w2_gate, w2_up, w2_down,
                    mix=(y_a, y_b, bf(w_o)))
    return out.reshape(batch, seq, d)
```

```python
import math

import jax
import jax.numpy as jnp
from jax import lax
from jax.experimental import pallas as pl
from jax.experimental.pallas import tpu as pltpu

F32 = jnp.float32
BF16 = jnp.bfloat16

RMS_EPS = 1e-6
MLA_HEADS = 8
MLA_NOPE = 64
MLA_ROPE = 32
MLA_V = 64
ROPE_THETA = 10000.0
SWA_HEADS = 8
SWA_KV_HEADS = 2
SWA_HEAD_DIM = 64
SWA_GROUP = SWA_HEADS // SWA_KV_HEADS
WINDOW = 128
LOG2E = math.log2(math.e)

LANES = 128
V7X_VMEM_BYTES = 64 * 1024 * 1024
VMEM_HEADROOM_BYTES = 8 * 1024 * 1024

FFN_ROWS = 512
FFN_ROW_PARTS = 1
FFN_CHUNK = 256
PROJ_ROWS = 1024
PROJ_CHUNKS = 1
MLA_ROWS = 1024
MLA_ROW_PARTS = 1
MLA_KEY_CHUNK = 512
SWA_ROWS = 1024
SWA_Q_ROWS = 128
SWA_KEYS = SWA_Q_ROWS + 2 * WINDOW


def _vmem_limit(block_bytes):
    want = 2 * block_bytes + 16 * 1024 * 1024
    return int(min(want, V7X_VMEM_BYTES - VMEM_HEADROOM_BYTES))


def _nbytes(shape, dtype):
    return math.prod(shape) * jnp.dtype(dtype).itemsize


def _const_spec(shape):
    zeros = (0,) * len(shape)
    return pl.BlockSpec(shape, lambda *_: zeros, pipeline_mode=pl.Buffered(1))


def _rms(x, gain):
    ms = jnp.mean(x * x, axis=-1, keepdims=True)
    return x * lax.rsqrt(ms + RMS_EPS) * gain


def _dot(a, b):
    return jnp.dot(a, b, preferred_element_type=F32)


def _dot_nt(a, b):
    return lax.dot_general(a, b, (((1,), (1,)), ((), ())), preferred_element_type=F32)


def _row_parts(rows):
    part = rows // FFN_ROW_PARTS
    return [slice(i * part, (i + 1) * part) for i in range(FFN_ROW_PARTS)]


def _swiglu_residual(xs, parts, gain_ref, wg_ref, wu_ref, wd_ref, act_ref, o_ref):
    hs = [_rms(x, gain_ref[...]) for x in xs]
    d_ff = wg_ref.shape[1]
    for x, h, rows in zip(xs, hs, parts):
        for c in range(d_ff // FFN_CHUNK):
            cols = slice(c * FFN_CHUNK, (c + 1) * FFN_CHUNK)
            gate = _dot(h, wg_ref[:, cols])
            up = _dot(h, wu_ref[:, cols])
            act_ref[rows, cols] = gate / (1.0 + jnp.exp(-gate)) * up
        o_ref[rows, :] = x + 0.5 * _dot(act_ref[rows, :], wd_ref[...])


def _ffn_kernel(x_ref, gain_ref, wg_ref, wu_ref, wd_ref, o_ref, act_ref):
    parts = _row_parts(x_ref.shape[0])
    xs = [x_ref[rows, :] for rows in parts]
    _swiglu_residual(xs, parts, gain_ref, wg_ref, wu_ref, wd_ref, act_ref, o_ref)


def _mix_ffn_kernel(x_ref, ya_ref, yb_ref, wo_ref, gain_ref, wg_ref, wu_ref, wd_ref,
                    o_ref, act_ref):
    half = ya_ref.shape[1]
    parts = _row_parts(x_ref.shape[0])
    xs = [x_ref[rows, :] + _dot(ya_ref[rows, :], wo_ref[:half, :])
          + _dot(yb_ref[rows, :], wo_ref[half:, :]) for rows in parts]
    _swiglu_residual(xs, parts, gain_ref, wg_ref, wu_ref, wd_ref, act_ref, o_ref)


def _ffn_call(x, gain, wg, wu, wd, mix=None):
    t, d = x.shape
    d_ff = wg.shape[1]
    assert t % FFN_ROWS == 0 and d_ff % FFN_CHUNK == 0
    row = lambda i: (i, 0)
    x_spec = pl.BlockSpec((FFN_ROWS, d), row)
    w_specs = [_const_spec((1, d)), _const_spec((d, d_ff)), _const_spec((d, d_ff)),
               _const_spec((d_ff, d))]
    block_bytes = (2 * _nbytes((FFN_ROWS, d), F32) + 3 * _nbytes((d, d_ff), BF16)
                   + _nbytes((FFN_ROWS, d_ff), BF16))
    if mix is None:
        body, operands, in_specs = _ffn_kernel, (x, gain, wg, wu, wd), [x_spec] + w_specs
    else:
        ya, yb, wo = mix
        half = ya.shape[1]
        y_spec = pl.BlockSpec((FFN_ROWS, half), row)
        body = _mix_ffn_kernel
        operands = (x, ya, yb, wo, gain, wg, wu, wd)
        in_specs = [x_spec, y_spec, y_spec, _const_spec(wo.shape)] + w_specs
        block_bytes += 2 * _nbytes((FFN_ROWS, half), BF16) + _nbytes(wo.shape, BF16)
    return pl.pallas_call(
        body,
        out_shape=jax.ShapeDtypeStruct((t, d), F32),
        grid=(t // FFN_ROWS,),
        in_specs=in_specs,
        out_specs=x_spec,
        scratch_shapes=[pltpu.VMEM((FFN_ROWS, d_ff), F32)],
        compiler_params=pltpu.CompilerParams(
            dimension_semantics=("parallel",), vmem_limit_bytes=_vmem_limit(block_bytes)),
        name="ffn" if mix is None else "mix_ffn",
    )(*operands)


def _segment_mean_sq(x, seg_ref):
    return _dot((x * x).astype(BF16), seg_ref[...])


def _proj_kernel(x_ref, gmix_ref, win_ref, gqa_ref, wuq_ref, gkva_ref, wk_ref, wv_ref,
                 vones_ref, gq_ref, gk_ref, gkr_ref, gsq_ref, gsk_ref,
                 seg_mla_ref, seg_swa_ref, rc_ref, ra_ref, rb_ref,
                 q_ref, k_ref, v_ref, sq_ref, sk_ref, sv_ref):
    pairs = range(MLA_HEADS // 2)
    grp = lambda i: slice(i * 2 * LANES, (i + 1) * 2 * LANES)

    def reduce_stage(rows):
        h = _rms(x_ref[rows, :], gmix_ref[...]).astype(BF16)
        rc, ra, rb = rc_ref[rows, :], ra_ref[rows, :], rb_ref[rows, :]

        def rope(t):
            return t * rc + pltpu.roll(t, LANES - 16, 1) * ra + pltpu.roll(t, 16, 1) * rb

        c_q = _rms(_dot(h, win_ref[:, 0:256]), gqa_ref[...]).astype(BF16)
        c_kv = _rms(_dot(h, win_ref[:, 256:384]), gkva_ref[...]).astype(BF16)
        hkr = _dot(h, win_ref[:, 384:512])
        kr_ms = jnp.sum(hkr * hkr, axis=-1, keepdims=True) * (1.0 / MLA_ROPE)
        k_pe = rope(hkr * lax.rsqrt(kr_ms + RMS_EPS) * gkr_ref[...])
        return dict(rows=rows, h=h, c_q=c_q, c_kv=c_kv, k_pe=k_pe, rope=rope)

    def raw_stage(st):
        h, c_q, c_kv, rows = st["h"], st["c_q"], st["c_kv"], st["rows"]
        st["q_raw"] = [_dot(c_q, wuq_ref[:, grp(i)]) for i in pairs]
        st["k_raw"] = [_dot(c_kv, wk_ref[:, grp(i)]) for i in pairs]
        st["s_raw"] = [_dot(h, win_ref[:, 512 + i * 2 * LANES:512 + (i + 1) * 2 * LANES])
                       for i in range(3)]
        sv_ref[rows, :] = _dot(h, win_ref[:, 1280:1536]).astype(BF16)
        for i in pairs:
            v_ref[rows, grp(i)] = (_dot(c_kv, wv_ref[:, grp(i)]) + vones_ref[:, grp(i)]).astype(BF16)

    def stats_stage(st):
        st["q_ms"] = [_segment_mean_sq(t, seg_mla_ref) for t in st["q_raw"]]
        st["k_ms"] = [_segment_mean_sq(t, seg_mla_ref) for t in st["k_raw"]]
        st["s_ms"] = [_segment_mean_sq(t, seg_swa_ref) for t in st["s_raw"]]

    def store_stage(st):
        rows, rope = st["rows"], st["rope"]
        for i in pairs:
            q = st["q_raw"][i] * lax.rsqrt(st["q_ms"][i] + RMS_EPS) * gq_ref[:, grp(i)]
            k = st["k_raw"][i] * lax.rsqrt(st["k_ms"][i] + RMS_EPS) * gk_ref[:, grp(i)]
            for j in range(2):
                blk = slice(j * LANES, (j + 1) * LANES)
                dst = slice((2 * i + j) * LANES, (2 * i + j + 1) * LANES)
                q_ref[rows, dst] = rope(q[:, blk]).astype(BF16)
                k_ref[rows, dst] = (k[:, blk] + st["k_pe"]).astype(BF16)
        s_raw, s_ms = st["s_raw"], st["s_ms"]
        for i in range(2):
            sq_ref[rows, grp(i)] = (s_raw[i] * lax.rsqrt(s_ms[i] + RMS_EPS) * gsq_ref[:, grp(i)]).astype(BF16)
        sk_ref[rows, :] = (s_raw[2] * lax.rsqrt(s_ms[2] + RMS_EPS) * gsk_ref[...]).astype(BF16)

    chunk = PROJ_ROWS // PROJ_CHUNKS
    stages = (raw_stage, stats_stage, store_stage)
    states = []
    for step in range(PROJ_CHUNKS + len(stages)):
        if step < PROJ_CHUNKS:
            states.append(reduce_stage(slice(step * chunk, (step + 1) * chunk)))
        for age, stage in enumerate(stages):
            c = step - 1 - age
            if 0 <= c < PROJ_CHUNKS:
                stage(states[c])


def _proj_call(x1, batch, seq, consts):
    t, d = x1.shape
    assert seq % PROJ_ROWS == 0
    steps = seq // PROJ_ROWS
    row = lambda j, b: (b * steps + j, 0)
    pos = lambda j, b: (j, 0)
    in_specs = [pl.BlockSpec((PROJ_ROWS, d), row)]
    in_specs += [_const_spec(c.shape) for c in consts[:-3]]
    in_specs += [pl.BlockSpec((PROJ_ROWS, LANES), pos)] * 3
    widths = (MLA_HEADS * LANES, MLA_HEADS * LANES, MLA_HEADS * LANES,
              SWA_HEADS * SWA_HEAD_DIM, 2 * LANES, 2 * LANES)
    out_shape = [jax.ShapeDtypeStruct((t, w), BF16) for w in widths]
    out_specs = [pl.BlockSpec((PROJ_ROWS, w), row) for w in widths]
    block_bytes = (_nbytes((PROJ_ROWS, d), F32) + sum(_nbytes(c.shape, c.dtype) for c in consts[:-3])
                   + 3 * _nbytes((PROJ_ROWS, LANES), F32)
                   + sum(_nbytes((PROJ_ROWS, w), BF16) for w in widths))
    return pl.pallas_call(
        _proj_kernel,
        out_shape=out_shape,
        grid=(steps, batch),
        in_specs=in_specs,
        out_specs=out_specs,
        compiler_params=pltpu.CompilerParams(
            dimension_semantics=("parallel", "parallel"),
            vmem_limit_bytes=_vmem_limit(block_bytes)),
        name="proj",
    )(x1, *consts)


def _mla_kernel(q_ref, k_ref, v_ref, o_ref):
    seq = k_ref.shape[0]
    lane = lax.broadcasted_iota(jnp.int32, (q_ref.shape[0], LANES), 1)
    outs = {}
    state = {}

    def scores(unit):
        h, c = unit
        blk = slice(h * LANES, (h + 1) * LANES)
        return _dot_nt(q_ref[:, blk], k_ref[c:c + MLA_KEY_CHUNK, blk])

    def finish(unit, s):
        h, c = unit
        v = v_ref[c:c + MLA_KEY_CHUNK, h * LANES:(h + 1) * LANES]
        m_chunk = jnp.max(s, axis=-1, keepdims=True)
        if c == 0:
            m = m_chunk
            acc = _dot(jnp.exp2(s - m).astype(BF16), v)
        else:
            m_old, acc_old = state[h]
            m = jnp.maximum(m_old, m_chunk)
            acc = acc_old * jnp.exp2(m_old - m) + _dot(jnp.exp2(s - m).astype(BF16), v)
        state[h] = (m, acc)
        if c + MLA_KEY_CHUNK == seq:
            outs[h] = acc / pltpu.roll(acc, LANES // 2, 1)
            if h % 2 == 1:
                o_ref[:, (h // 2) * LANES:(h // 2 + 1) * LANES] = (
                    jnp.where(lane < LANES // 2, outs[h - 1], outs[h]).astype(BF16))

    units = [(h, c) for h in range(MLA_HEADS) for c in range(0, seq, MLA_KEY_CHUNK)]
    s = scores(units[0])
    for i, unit in enumerate(units):
        s_next = scores(units[i + 1]) if i + 1 < len(units) else None
        finish(unit, s)
        s = s_next


def _mla_call(q, k, v, batch, seq):
    t, width = q.shape
    assert seq % MLA_ROWS == 0
    steps = seq // MLA_ROWS
    q_spec = pl.BlockSpec((MLA_ROWS, width), lambda b, i: (b * steps + i, 0))
    kv_spec = pl.BlockSpec((seq, width), lambda b, i: (b, 0))
    o_spec = pl.BlockSpec((MLA_ROWS, MLA_HEADS * MLA_V), lambda b, i: (b * steps + i, 0))
    block_bytes = (2 * _nbytes((seq, width), BF16) + 2 * _nbytes((MLA_ROWS, width), BF16)
                   + 4 * _nbytes((MLA_ROWS, seq), F32))
    return pl.pallas_call(
        _mla_kernel,
        out_shape=jax.ShapeDtypeStruct((t, MLA_HEADS * MLA_V), BF16),
        grid=(batch, steps),
        in_specs=[q_spec, kv_spec, kv_spec],
        out_specs=o_spec,
        compiler_params=pltpu.CompilerParams(
            dimension_semantics=("parallel", "parallel"),
            vmem_limit_bytes=_vmem_limit(block_bytes)),
        name="mla_attn",
    )(q, k, v)


def _swa_key_start(i, seq):
    return jnp.clip(i * SWA_Q_ROWS - WINDOW, 0, seq - SWA_KEYS)


def _swa_bias_tables(seq):
    steps = seq // SWA_Q_ROWS
    r = jnp.arange(SWA_Q_ROWS)[:, None]
    c = jnp.arange(SWA_KEYS)[None, :]
    slopes = 2.0 ** (-(8.0 / SWA_HEADS) * jnp.arange(1, SWA_HEADS + 1, dtype=F32))
    cases = []
    for i in (0, 1, steps - 1):
        dist = jnp.abs(c - r + (_swa_key_start(i, seq) - i * SWA_Q_ROWS)).astype(F32)
        bias = -LOG2E * slopes[:, None, None] * dist[None]
        cases.append(jnp.where((dist <= WINDOW)[None], bias, -jnp.inf))
    return jnp.stack(cases)


def _swa_stages(sink_ref, bias_ref, q_ref, k_ref, v_ref, o_ref):
    seq = k_ref.shape[0]
    blocks = seq // SWA_Q_ROWS
    rows = SWA_Q_ROWS
    blocks_per_step = q_ref.shape[0] // rows
    lane = lax.broadcasted_iota(jnp.int32, (rows, LANES), 1)
    low_half = lane < LANES // 2
    kv_lane = lax.broadcasted_iota(jnp.int32, (SWA_KEYS, 2 * LANES), 1)
    low_kv = (kv_lane & (LANES // 2)) == 0

    pairs = range(SWA_GROUP // 2)
    q_block = lambda g, pr: slice((g * (SWA_GROUP // 2) + pr) * LANES,
                                  (g * (SWA_GROUP // 2) + pr + 1) * LANES)

    def score_units(u):
        blk = pl.program_id(1) * blocks_per_step + u
        key0 = pl.multiple_of(_swa_key_start(blk, seq), LANES)
        case = jnp.where(blk == 0, 0, jnp.where(blk == blocks - 1, 2, 1))
        qrows = slice(u * rows, (u + 1) * rows)
        kwin = k_ref[pl.ds(key0, SWA_KEYS), :].astype(F32)
        vwin = v_ref[pl.ds(key0, SWA_KEYS), :].astype(F32)
        k_lo = jnp.where(low_kv, kwin, 0.0).astype(BF16)
        k_hi = jnp.where(low_kv, 0.0, kwin).astype(BF16)
        v_lo = jnp.where(low_kv, vwin, 0.0).astype(BF16)
        v_hi = jnp.where(low_kv, 0.0, vwin).astype(BF16)
        units = []
        for g in range(SWA_KV_HEADS):
            grp = slice(g * LANES, (g + 1) * LANES)
            k_bd = jnp.concatenate([k_lo[:, grp], k_hi[:, grp]], axis=0)
            v_bd = jnp.concatenate([v_lo[:, grp], v_hi[:, grp]], axis=0)
            q2 = jnp.concatenate([q_ref[qrows, q_block(g, pr)] for pr in pairs], axis=0)
            units.append((g, qrows, case, _dot_nt(q2, k_bd), v_bd))
        return units

    def finish(unit):
        g, qrows, case, scores, v_bd = unit
        p_rows, denoms = [], []
        for pr in pairs:
            p_halves, d_halves = [], []
            for half in range(2):
                head = g * SWA_GROUP + 2 * pr + half
                logits = (scores[pr * rows:(pr + 1) * rows, half * SWA_KEYS:(half + 1) * SWA_KEYS]
                          + bias_ref[case, head])
                sink = sink_ref[head] * LOG2E
                m = jnp.maximum(jnp.max(logits, axis=-1, keepdims=True), sink)
                p = jnp.exp2(logits - m)
                d_halves.append(jnp.sum(p, axis=-1, keepdims=True) + jnp.exp2(sink - m))
                p_halves.append(p.astype(BF16))
            p_rows.append(jnp.concatenate(p_halves, axis=1))
            denoms.append(jnp.where(low_half, d_halves[0], d_halves[1]))
        o = _dot(jnp.concatenate(p_rows, axis=0), v_bd)
        for pr in pairs:
            o_ref[qrows, q_block(g, pr)] = (o[pr * rows:(pr + 1) * rows] / denoms[pr]).astype(BF16)

    return score_units, finish


def _swa_kernel(sink_ref, bias_ref, q_ref, k_ref, v_ref, o_ref):
    score_units, finish = _swa_stages(sink_ref, bias_ref, q_ref, k_ref, v_ref, o_ref)
    blocks = q_ref.shape[0] // SWA_Q_ROWS
    pending = score_units(0)
    for u in range(blocks):
        current = pending
        if u + 1 < blocks:
            pending = score_units(u + 1)
        for unit in current:
            finish(unit)


def _swa_call(sink, sq, sk, sv, batch, seq):
    t, width = sq.shape
    assert seq % SWA_ROWS == 0 and SWA_ROWS % SWA_Q_ROWS == 0
    assert seq >= SWA_KEYS and seq // SWA_Q_ROWS >= 3
    steps = seq // SWA_ROWS
    q_spec = pl.BlockSpec((SWA_ROWS, width), lambda b, i: (b * steps + i, 0))
    kv_spec = pl.BlockSpec((seq, 2 * LANES), lambda b, i: (b, 0))
    bias = _swa_bias_tables(seq)
    block_bytes = (2 * _nbytes((seq, 2 * LANES), BF16) + 2 * _nbytes((SWA_ROWS, width), BF16)
                   + _nbytes(bias.shape, F32) + 4 * _nbytes((SWA_ROWS, 2 * SWA_KEYS), F32))
    return pl.pallas_call(
        _swa_kernel,
        out_shape=jax.ShapeDtypeStruct((t, width), BF16),
        grid=(batch, steps),
        in_specs=[pl.BlockSpec(memory_space=pltpu.SMEM), _const_spec(bias.shape),
                  q_spec, kv_spec, kv_spec],
        out_specs=q_spec,
        compiler_params=pltpu.CompilerParams(
            dimension_semantics=("parallel", "parallel"),
            vmem_limit_bytes=_vmem_limit(block_bytes)),
        name="swa_attn",
    )(sink, bias, sq, sk, sv)


def _head_blocks(w, heads, width, lo, hi, at=0):
    rows = w.shape[0]
    part = w.reshape(rows, heads, width)[:, :, lo:hi]
    part = jnp.pad(part, ((0, 0), (0, 0), (at, LANES - at - (hi - lo))))
    return part.reshape(rows, heads * LANES)


def _proj_constants(seq, g_mix, w_in, g_q_a, w_uq, g_kv_a, w_ukv,
                    g_qn, g_qr, g_kn, g_kr, g_swa_q, g_swa_k):
    d = w_in.shape[0]
    q_rank, kv_rank = g_q_a.shape[0], g_kv_a.shape[0]
    o_kr = q_rank + kv_rank
    o_sq = o_kr + MLA_ROPE
    o_sk = o_sq + SWA_HEADS * SWA_HEAD_DIM
    o_sv = o_sk + SWA_KV_HEADS * SWA_HEAD_DIM
    kr_block = jnp.pad(w_in[:, o_kr:o_sq], ((0, 0), (MLA_NOPE, LANES - MLA_NOPE - MLA_ROPE)))
    dup = lambda w: jnp.repeat(w.reshape(d, SWA_KV_HEADS, 1, SWA_HEAD_DIM), 2, axis=2).reshape(d, -1)
    win = jnp.concatenate(
        [w_in[:, :o_kr], kr_block, w_in[:, o_sq:o_sk], dup(w_in[:, o_sk:o_sv]), dup(w_in[:, o_sv:])],
        axis=1).astype(BF16)

    qk_width = MLA_NOPE + MLA_ROPE
    wuq = _head_blocks(w_uq, MLA_HEADS, qk_width, 0, qk_width).astype(BF16)
    kv_width = MLA_NOPE + MLA_V
    wk = _head_blocks(w_ukv, MLA_HEADS, kv_width, 0, MLA_NOPE).astype(BF16)
    wv_even = _head_blocks(w_ukv, MLA_HEADS, kv_width, MLA_NOPE, kv_width, at=0)
    wv_odd = _head_blocks(w_ukv, MLA_HEADS, kv_width, MLA_NOPE, kv_width, at=LANES // 2)
    head_of_col = jnp.arange(MLA_HEADS * LANES) // LANES
    odd_col = (head_of_col % 2 == 1)[None, :]
    wv = jnp.where(odd_col, wv_odd, wv_even).astype(BF16)
    lane = jnp.arange(MLA_HEADS * LANES) % LANES
    vones = ((lane >= LANES // 2) != odd_col[0]).astype(F32)[None, :]

    scale = LOG2E / math.sqrt(qk_width)
    zeros = lambda n: jnp.zeros((n,), F32)
    gq = jnp.tile(jnp.concatenate([g_qn, g_qr, zeros(LANES - qk_width)]) * scale, MLA_HEADS)[None, :]
    gk = jnp.tile(jnp.concatenate([g_kn, zeros(LANES - MLA_NOPE)]), MLA_HEADS)[None, :]
    gkr = jnp.concatenate([zeros(MLA_NOPE), g_kr, zeros(LANES - qk_width)])[None, :]
    gsq = jnp.tile(g_swa_q * (LOG2E / math.sqrt(SWA_HEAD_DIM)), SWA_HEADS)[None, :]
    gsk = jnp.tile(g_swa_k, 2 * SWA_KV_HEADS)[None, :]

    idx = jnp.arange(2 * LANES)
    in_blk = idx % LANES
    seg_id = jnp.where(in_blk < MLA_NOPE, 0, jnp.where(in_blk < qk_width, 1, 2)) + 3 * (idx // LANES)
    seg_len = jnp.where(in_blk < MLA_NOPE, MLA_NOPE, MLA_ROPE).astype(F32)
    same = (seg_id[:, None] == seg_id[None, :]) & (in_blk < qk_width)[:, None]
    seg_mla = jnp.where(same, 1.0 / seg_len[None, :], 0.0).astype(BF16)
    seg_swa = jnp.where((idx // SWA_HEAD_DIM)[:, None] == (idx // SWA_HEAD_DIM)[None, :],
                        1.0 / SWA_HEAD_DIM, 0.0).astype(BF16)

    pos = jnp.arange(seq, dtype=F32)
    inv = 1.0 / (ROPE_THETA ** (jnp.arange(0, MLA_ROPE, 2, dtype=F32) / MLA_ROPE))
    ang = pos[:, None] * inv[None, :]
    cos, sin = jnp.cos(ang), jnp.sin(ang)
    half = MLA_ROPE // 2
    pad_lo = jnp.ones((seq, MLA_NOPE), F32)
    pad_hi = jnp.ones((seq, LANES - qk_width), F32)
    rc = jnp.concatenate([pad_lo, cos, cos, pad_hi], axis=1)
    ra = jnp.concatenate([0 * pad_lo, -sin, jnp.zeros((seq, half), F32), 0 * pad_hi], axis=1)
    rb = jnp.concatenate([0 * pad_lo, jnp.zeros((seq, half), F32), sin, 0 * pad_hi], axis=1)

    row = lambda g: g.astype(F32)[None, :]
    return (row(g_mix), win, row(g_q_a), wuq, row(g_kv_a), wk, wv, vones,
            gq, gk, gkr, gsq, gsk, seg_mla, seg_swa, rc, ra, rb)


def kernel(x, g_ffn1, w1_gate, w1_up, w1_down, g_mix, w_in, g_q_a, w_uq, g_kv_a, w_ukv,
           g_mla_qn, g_mla_qr, g_mla_kn, g_mla_kr, g_swa_q, g_swa_k, sink, w_o,
           g_ffn2, w2_gate, w2_up, w2_down):
    batch, seq, d = x.shape
    x2d = x.reshape(batch * seq, d)
    bf = lambda w: w.astype(BF16)
    row = lambda g: g.astype(F32)[None, :]

    x1 = _ffn_call(x2d, row(g_ffn1), w1_gate, w1_up, w1_down)
    consts = _proj_constants(seq, g_mix, w_in, g_q_a, w_uq, g_kv_a, w_ukv,
                             g_mla_qn, g_mla_qr, g_mla_kn, g_mla_kr, g_swa_q, g_swa_k)
    q, k, v, sq, sk, sv = _proj_call(x1, batch, seq, consts)
    y_a = _mla_call(q, k, v, batch, seq)
    y_b = _swa_call(sink.astype(F32), sq, sk, sv, batch, seq)
    out = _ffn_call(x1, row(g_ffn2), w2_gate, w2_up, w2_down,
                    mix=(y_a, y_b, bf(w_o)))
    return out.reshape(batch, seq, d)
```
